```python
import math
import jax, jax.numpy as jnp
from jax import lax
import numpy as np

D_MODEL = 1024
BATCH = 32
SEQ = 2048
DEPTH = 2
DEC_BATCH = 8
DEC_SEQ = 64
PAST_LEN = 4096

CHUNK = 64
Q_BLOCK = 128
MIX_WIDTH = D_MODEL
A_WIDTH = MIX_WIDTH // 2
B_WIDTH = MIX_WIDTH - A_WIDTH
HA = 4
HDA = A_WIDTH // (2 * HA)
HB = 4
DKB = B_WIDTH // HB
DVB = B_WIDTH // HB
ROPE_THETA = 10000.0
N_GROUPS = 4
EXPERTS_PER_GROUP = 8
TOP_K = 2
D_EXPERT = D_MODEL // 4
EPS = 1e-6
SPLIT_WIDTHS = (2 * HA * HDA, 2 * HA * HDA, HA * 2 * HDA, HB * DKB, HB * DKB, HB * DVB, B_WIDTH)
IN_COLS = sum(SPLIT_WIDTHS)

kernel_name = "hybrid_diffattn_retention_hiermoe_stream_step"


def _rms_norm(x, g):
    xf = x.astype(jnp.float32)
    y = xf * lax.rsqrt(jnp.mean(xf * xf, axis=-1, keepdims=True) + EPS)
    return (y * g.astype(jnp.float32)).astype(x.dtype)


def _rotary(x, pos):
    d = x.shape[-1]
    inv = ROPE_THETA ** (-jnp.arange(0, d, 2, dtype=jnp.float32) / d)
    ang = pos.astype(jnp.float32)[:, None] * inv[None, :]
    cos = jnp.cos(ang)[:, None, :]
    sin = jnp.sin(ang)[:, None, :]
    xf = x.astype(jnp.float32)
    x1, x2 = xf[..., : d // 2], xf[..., d // 2:]
    return jnp.concatenate([x1 * cos - x2 * sin, x2 * cos + x1 * sin], axis=-1).astype(x.dtype)


def _log_gamma():
    return jnp.log(1.0 - 2.0 ** (-5.0 - jnp.arange(HB, dtype=jnp.float32)))


def _project(h, pos, w_in, gq, gk):
    B, S, _ = h.shape
    z = h @ w_in
    cuts = list(np.cumsum(SPLIT_WIDTHS)[:-1])
    qa, ka, va, qb, kb, vb, gb = jnp.split(z, cuts, axis=-1)
    qa = _rotary(_rms_norm(qa.reshape(B, S, 2 * HA, HDA), gq), pos)
    ka = _rotary(_rms_norm(ka.reshape(B, S, 2 * HA, HDA), gk), pos)
    va = va.reshape(B, S, HA, 2 * HDA)
    qb = _rotary(qb.reshape(B, S, HB, DKB), pos).transpose(0, 2, 1, 3)
    kb = (_rotary(kb.reshape(B, S, HB, DKB), pos) * (DKB ** -0.5)).transpose(0, 2, 1, 3)
    vb = vb.reshape(B, S, HB, DVB).transpose(0, 2, 1, 3)
    return qa, ka, va, qb, kb, vb, gb


def _diff_lambda(lam_vec, lam_init):
    lv = lam_vec.astype(jnp.float32)
    return jnp.exp(jnp.sum(lv[0] * lv[1])) - jnp.exp(jnp.sum(lv[2] * lv[3])) + lam_init


def _diff_attn_block(q, k, v, lam, mask):
    B, Sq = q.shape[0], q.shape[1]
    Sk = k.shape[1]
    s = jnp.einsum('bqhd,bkhd->bhqk', q, k).astype(jnp.float32) * (HDA ** -0.5)
    if mask is not None:
        s = jnp.where(mask[None, None], s, -jnp.inf)
    p = jax.nn.softmax(s, axis=-1).reshape(B, HA, 2, Sq, Sk)
    a = p[:, :, 0] - lam * p[:, :, 1]
    return jnp.einsum('bhqk,bkhe->bqhe', a.astype(v.dtype), v)


def _diff_attn_prompt(q, k, v, lam):
    S = q.shape[1]
    outs = []
    for blk in range(S // Q_BLOCK):
        q0, q1 = blk * Q_BLOCK, (blk + 1) * Q_BLOCK
        qpos = jnp.arange(q0, q1)
        kpos = jnp.arange(q1)
        mask = (kpos[None, :] // CHUNK) <= (qpos[:, None] // CHUNK)
        outs.append(_diff_attn_block(q[:, q0:q1], k[:, :q1], v[:, :q1], lam, mask))
    return jnp.concatenate(outs, axis=1)


def _retention_chunk(q, k, v, state, log_gamma):
    L = q.shape[2]
    idx = jnp.arange(L, dtype=jnp.float32)
    diff = idx[:, None] - idx[None, :]
    decay = jnp.where(diff >= 0, jnp.exp(log_gamma[:, None, None] * jnp.maximum(diff, 0.0)), 0.0)
    qf, kf, vf = q.astype(jnp.float32), k.astype(jnp.float32), v.astype(jnp.float32)
    inner = jnp.einsum('bhld,bhmd->bhlm', qf, kf) * decay
    xi = jnp.exp(log_gamma[:, None] * (idx + 1.0))
    out = jnp.einsum('bhlm,bhme->bhle', inner, vf) + jnp.einsum('bhld,bhde->bhle', qf, state) * xi[None, :, :, None]
    zeta = jnp.exp(log_gamma[:, None] * (L - 1.0 - idx))
    new_state = jnp.exp(log_gamma * L)[None, :, None, None] * state + jnp.einsum('bhld,bhle->bhde', kf * zeta[None, :, :, None], vf)
    return out, new_state


def _retention_prompt(q, k, v, log_gamma):
    B, H, S, _ = q.shape
    n = S // CHUNK

    def split(t):
        return t.reshape(B, H, n, CHUNK, t.shape[-1]).transpose(2, 0, 1, 3, 4)

    def step(state, inp):
        qc, kc, vc = inp
        o, state = _retention_chunk(qc, kc, vc, state, log_gamma)
        return state, o

    state0 = jnp.zeros((B, H, DKB, DVB), jnp.float32)
    state, o = lax.scan(step, state0, (split(q), split(k), split(v)))
    o = o.transpose(1, 2, 0, 3, 4).reshape(B, H, S, DVB)
    return o, state


def _merge(oa, ob, gb, subln_g, ret_g, lam_init):
    B, S = oa.shape[0], oa.shape[1]
    ya = _rms_norm(oa, subln_g) * (1.0 - lam_init)
    yb = _rms_norm(ob.transpose(0, 2, 1, 3).astype(gb.dtype), ret_g) * jax.nn.silu(gb).reshape(B, S, HB, DVB)
    return jnp.concatenate([ya.reshape(B, S, A_WIDTH), yb.reshape(B, S, B_WIDTH)], axis=-1)


def _hier_moe(h, w_group, b_group, w_expert, b_expert, w_gate, w_up, w_down):
    B, S, D = h.shape
    t = h.reshape(B * S, D)
    T = t.shape[0]
    g_logits = (t @ w_group).astype(jnp.float32) + b_group.astype(jnp.float32)
    g_prob = jax.nn.softmax(g_logits, axis=-1)
    g_sel = jnp.argmax(g_logits, axis=-1)
    g_w = jnp.take_along_axis(g_prob, g_sel[:, None], axis=1)[:, 0]
    e_logits = ((t @ w_expert).astype(jnp.float32) + b_expert.astype(jnp.float32)).reshape(T, N_GROUPS, EXPERTS_PER_GROUP)
    e_sel_logits = jnp.take_along_axis(e_logits, g_sel[:, None, None], axis=1)[:, 0]
    e_prob = jax.nn.softmax(e_sel_logits, axis=-1)
    top_v, top_i = lax.top_k(e_prob, TOP_K)
    top_v = top_v / jnp.sum(top_v, axis=-1, keepdims=True)
    comb_e = jnp.sum(jax.nn.one_hot(top_i, EXPERTS_PER_GROUP, dtype=jnp.float32) * top_v[..., None], axis=1)
    comb = jax.nn.one_hot(g_sel, N_GROUPS, dtype=jnp.float32)[:, :, None] * (g_w[:, None] * comb_e)[:, None, :]
    y = jnp.zeros((T, D), jnp.float32)
    for gi in range(N_GROUPS):
        for ei in range(EXPERTS_PER_GROUP):
            act = jax.nn.silu(t @ w_gate[gi, ei]) * (t @ w_up[gi, ei])
            y = y + comb[:, gi, ei, None] * (act @ w_down[gi, ei]).astype(jnp.float32)
    return y.astype(h.dtype).reshape(B, S, D)


def setup_inputs(seed: int = 0) -> dict:
    key = jax.random.key(seed)
    ks = jax.random.split(key, 24)
    f32 = jnp.float32
    nrm = lambda k, shape, s: (jax.random.normal(k, shape, f32) * s).astype(f32)
    G, E = N_GROUPS, EXPERTS_PER_GROUP
    return {
        'x_prompt': nrm(ks[0], (BATCH, SEQ, D_MODEL), 1.0),
        'x_sample': nrm(ks[1], (DEC_BATCH, DEC_SEQ, D_MODEL), 1.0),
        'cache_k': nrm(ks[2], (DEPTH, DEC_BATCH, PAST_LEN, 2 * HA, HDA), 1.0),
        'cache_v': nrm(ks[3], (DEPTH, DEC_BATCH, PAST_LEN, HA, 2 * HDA), 1.0),
        'state_ret': nrm(ks[4], (DEPTH, DEC_BATCH, HB, DKB, DVB), 0.5),
        'norm_attn': 1.0 + nrm(ks[5], (DEPTH, D_MODEL), 0.02),
        'w_in': nrm(ks[6], (DEPTH, D_MODEL, IN_COLS), D_MODEL ** -0.5),
        'q_norm': 1.0 + nrm(ks[7], (DEPTH, HDA), 0.02),
        'k_norm': 1.0 + nrm(ks[8], (DEPTH, HDA), 0.02),
        'lam_vec': nrm(ks[9], (DEPTH, 4, HDA), 0.1),
        'subln': 1.0 + nrm(ks[10], (DEPTH, 2 * HDA), 0.02),
        'ret_norm': 1.0 + nrm(ks[11], (DEPTH, DVB), 0.02),
        'w_out': nrm(ks[12], (DEPTH, MIX_WIDTH, D_MODEL), MIX_WIDTH ** -0.5),
        'norm_ffn': 1.0 + nrm(ks[13], (DEPTH, D_MODEL), 0.02),
        'w_group': nrm(ks[14], (DEPTH, D_MODEL, G), D_MODEL ** -0.5),
        'b_group': nrm(ks[15], (DEPTH, G), 0.01),
        'w_expert': nrm(ks[16], (DEPTH, D_MODEL, G * E), D_MODEL ** -0.5),
        'b_expert': nrm(ks[17], (DEPTH, G * E), 0.01),
        'w_gate': nrm(ks[18], (DEPTH, G, E, D_MODEL, D_EXPERT), D_MODEL ** -0.5),
        'w_up': nrm(ks[19], (DEPTH, G, E, D_MODEL, D_EXPERT), D_MODEL ** -0.5),
        'w_down': nrm(ks[20], (DEPTH, G, E, D_EXPERT, D_MODEL), D_EXPERT ** -0.5),
    }


def reference(x_prompt, x_sample, cache_k, cache_v, state_ret, norm_attn, w_in, q_norm, k_norm, lam_vec,
              subln, ret_norm, w_out, norm_ffn, w_group, b_group, w_expert, b_expert, w_gate, w_up, w_down):
    log_gamma = _log_gamma()
    pos_p = jnp.arange(x_prompt.shape[1])
    pos_s = PAST_LEN + jnp.arange(x_sample.shape[1])
    xp, xs = x_prompt, x_sample
    kp, vp, sp, ksm, vsm, ssm = [], [], [], [], [], []
    for l in range(DEPTH):
        lam_init = 0.8 - 0.6 * math.exp(-0.3 * l)
        lam = _diff_lambda(lam_vec[l], lam_init)

        qa, ka, va, qb, kb, vb, gb = _project(_rms_norm(xp, norm_attn[l]), pos_p, w_in[l], q_norm[l], k_norm[l])
        oa = _diff_attn_prompt(qa, ka, va, lam)
        ob, st = _retention_prompt(qb, kb, vb, log_gamma)
        xp = xp + _merge(oa, ob, gb, subln[l], ret_norm[l], lam_init) @ w_out[l]
        xp = xp + _hier_moe(_rms_norm(xp, norm_ffn[l]), w_group[l], b_group[l], w_expert[l], b_expert[l],
                            w_gate[l], w_up[l], w_down[l])
        kp.append(ka)
        vp.append(va)
        sp.append(st.astype(xp.dtype))

        qa, ka, va, qb, kb, vb, gb = _project(_rms_norm(xs, norm_attn[l]), pos_s, w_in[l], q_norm[l], k_norm[l])
        k_all = jnp.concatenate([cache_k[l].astype(ka.dtype), ka], axis=1)
        v_all = jnp.concatenate([cache_v[l].astype(va.dtype), va], axis=1)
        oa = _diff_attn_block(qa, k_all, v_all, lam, None)
        ob, st = _retention_chunk(qb, kb, vb, state_ret[l].astype(jnp.float32), log_gamma)
        xs = xs + _merge(oa, ob, gb, subln[l], ret_norm[l], lam_init) @ w_out[l]
        xs = xs + _hier_moe(_rms_norm(xs, norm_ffn[l]), w_group[l], b_group[l], w_expert[l], b_expert[l],
                            w_gate[l], w_up[l], w_down[l])
        ksm.append(ka)
        vsm.append(va)
        ssm.append(st.astype(xs.dtype))

    return (xp, xs, jnp.stack(kp), jnp.stack(vp), jnp.stack(sp), jnp.stack(ksm), jnp.stack(vsm), jnp.stack(ssm))
```

```python
import functools

import jax
import jax.numpy as jnp
import numpy as np
from jax import lax
from jax.experimental import pallas as pl
from jax.experimental.pallas import tpu as pltpu

F32 = jnp.float32
BF16 = jnp.bfloat16
I32 = jnp.int32

CHUNK = 64
ROPE_THETA = 10000.0
EPS = 1e-6
NEG = -1e30
LANE = 128
ROUTER_LANES = 128
VMEM_LIMIT_BYTES = 52 * 1024 * 1024

TOKEN_TILE = 512
ATTN_TILE = 256
RET_CHUNK = 256
EXPERT_TILE = 128


def _cparams(*sem):
    return pltpu.CompilerParams(dimension_semantics=sem, vmem_limit_bytes=VMEM_LIMIT_BYTES)


def _rms(x, gain):
    ms = jnp.mean(x * x, axis=-1, keepdims=True)
    return x * lax.rsqrt(ms + EPS) * gain


def _rot_half(y, half):
    if 2 * half == LANE:
        return pltpu.roll(y, half, axis=1)
    lane = lax.broadcasted_iota(I32, y.shape, 1)
    first = (lane & (2 * half - 1)) < half
    return jnp.where(first, pltpu.roll(y, LANE - half, axis=1), pltpu.roll(y, half, axis=1))


def _proj_kernel(x_ref, g_ref, w_ref, gm_ref, gq_ref, gk_ref, ca_ref, sa_ref, cb_ref, sb_ref,
                 qa_ref, ka_ref, va_ref, qb_ref, kb_ref, vb_ref, gb_ref, *, aw, bw, hda, dkb):
    h = _rms(x_ref[...], g_ref[...]).astype(BF16)

    def proj(c0, width):
        return jnp.dot(h, w_ref[:, c0:c0 + width], preferred_element_type=F32)

    def qk_norm_rot(z, gain_ref, out_ref):
        ss = jnp.dot((z * z).astype(BF16), gm_ref[...], preferred_element_type=F32)
        y = z * lax.rsqrt(ss + EPS) * gain_ref[...]
        for c in range(aw // LANE):
            yc = y[:, c * LANE:(c + 1) * LANE]
            r = yc * ca_ref[...] + _rot_half(yc, hda // 2) * sa_ref[...]
            out_ref[:, c * LANE:(c + 1) * LANE] = r.astype(out_ref.dtype)

    def rot_b(z, out_ref, scale):
        for c in range(bw // LANE):
            zc = z[:, c * LANE:(c + 1) * LANE]
            r = zc * cb_ref[...] + _rot_half(zc, dkb // 2) * sb_ref[...]
            out_ref[:, c * LANE:(c + 1) * LANE] = (r * scale).astype(out_ref.dtype)

    qk_norm_rot(proj(0, aw), gq_ref, qa_ref)
    qk_norm_rot(proj(aw, aw), gk_ref, ka_ref)
    va_ref[...] = proj(2 * aw, aw)
    rot_b(proj(3 * aw, bw), qb_ref, 1.0)
    rot_b(proj(3 * aw + bw, bw), kb_ref, dkb ** -0.5)
    vb_ref[...] = proj(3 * aw + 2 * bw, bw).astype(BF16)
    gb_ref[...] = proj(3 * aw + 3 * bw, bw).astype(BF16)


def _project(x2d, seq, gain, w_bf, gm, gq_t, gk_t, tabs, *, aw, bw, hda, dkb):
    t, d = x2d.shape
    ts = min(TOKEN_TILE, seq)
    n_s = seq // ts
    row = lambda i: (i, 0)
    const = lambda i: (0, 0)
    tab = lambda i: (i % n_s, 0)
    tab_spec = pl.BlockSpec((ts, LANE), tab)
    out_a = pl.BlockSpec((ts, aw), row)
    out_b = pl.BlockSpec((ts, bw), row)
    return pl.pallas_call(
        functools.partial(_proj_kernel, aw=aw, bw=bw, hda=hda, dkb=dkb),
        grid=(t // ts,),
        in_specs=[pl.BlockSpec((ts, d), row), pl.BlockSpec((1, d), const),
                  pl.BlockSpec(w_bf.shape, const), pl.BlockSpec(gm.shape, const),
                  pl.BlockSpec((1, aw), const), pl.BlockSpec((1, aw), const),
                  tab_spec, tab_spec, tab_spec, tab_spec],
        out_specs=[out_a, out_a, out_a, out_b, out_b, out_b, out_b],
        out_shape=[jax.ShapeDtypeStruct((t, aw), BF16), jax.ShapeDtypeStruct((t, aw), F32),
                   jax.ShapeDtypeStruct((t, aw), F32), jax.ShapeDtypeStruct((t, bw), BF16),
                   jax.ShapeDtypeStruct((t, bw), BF16), jax.ShapeDtypeStruct((t, bw), BF16),
                   jax.ShapeDtypeStruct((t, bw), BF16)],
        compiler_params=_cparams("parallel"),
        name="proj",
    )(x2d, gain, w_bf, gm, gq_t, gk_t, *tabs)


def _attn_kernel(*refs, causal, has_cache, tq, tk, n_new, n_cache, hda, lam_init):
    if has_cache:
        (q_ref, kn_ref, vn_ref, kc_ref, vc_ref, lv_ref, sg_ref, o_ref,
         klo_ref, khi_ref, vt_ref, klc_ref, khc_ref, vtc_ref, m_ref, l_ref, acc_ref) = refs
    else:
        (q_ref, kn_ref, vn_ref, lv_ref, sg_ref, o_ref,
         klo_ref, khi_ref, vt_ref, m_ref, l_ref, acc_ref) = refs
    qi = pl.program_id(2)
    tkn = klo_ref.shape[0] // n_new

    def stage(k_src, v_src, lo_dst, hi_dst, vt_dst, nblk, rows):
        k = k_src[0]
        lo = lax.broadcasted_iota(I32, k.shape, 1) < hda
        lo_dst[...] = jnp.where(lo, k, 0.0).astype(BF16)
        hi_dst[...] = jnp.where(lo, 0.0, k).astype(BF16)
        for j in range(nblk):
            vt_dst[j] = v_src[0, j * rows:(j + 1) * rows, :].T.astype(BF16)

    @pl.when(qi == 0)
    def _stage_keys():
        stage(kn_ref, vn_ref, klo_ref, khi_ref, vt_ref, n_new, tkn)
        if has_cache:
            stage(kc_ref, vc_ref, klc_ref, khc_ref, vtc_ref, n_cache, tk)

    q = q_ref[0] * (hda ** -0.5)
    m_ref[...] = jnp.full(m_ref.shape, NEG, F32)
    l_ref[...] = jnp.zeros(l_ref.shape, F32)
    acc_ref[...] = jnp.zeros(acc_ref.shape, F32)

    def block(k_halves, vt, valid):
        for j, kk in enumerate(k_halves):
            s = lax.dot_general(kk, q, (((1,), (1,)), ((), ())), preferred_element_type=F32)
            if valid is not None:
                s = jnp.where(valid, s, NEG)
            m_old = m_ref[j]
            m_new = jnp.maximum(m_old, jnp.max(s, axis=0, keepdims=True))
            alpha = jnp.exp(m_old - m_new)
            p = jnp.exp(s - m_new)
            l_ref[j] = alpha * l_ref[j] + jnp.sum(p, axis=0, keepdims=True)
            acc_ref[j] = alpha * acc_ref[j] + jnp.dot(vt, p.astype(BF16), preferred_element_type=F32)
            m_ref[j] = m_new

    if has_cache:
        def cache_body(kb, carry):
            r0 = pl.multiple_of(kb * tk, tk)
            block((klc_ref[pl.ds(r0, tk), :], khc_ref[pl.ds(r0, tk), :]), vtc_ref[kb], None)
            return carry
        lax.fori_loop(0, n_cache, cache_body, 0)

    if causal:
        def past_body(kb, carry):
            r0 = pl.multiple_of(kb * tk, tk)
            block((klo_ref[pl.ds(r0, tk), :], khi_ref[pl.ds(r0, tk), :]), vt_ref[kb], None)
            return carry
        lax.fori_loop(0, qi, past_body, 0)
        r0 = pl.multiple_of(qi * tk, tk)
        shift = CHUNK.bit_length() - 1
        kchunk = lax.broadcasted_iota(I32, (tk, tq), 0) >> shift
        qchunk = lax.broadcasted_iota(I32, (tk, tq), 1) >> shift
        block((klo_ref[pl.ds(r0, tk), :], khi_ref[pl.ds(r0, tk), :]), vt_ref[qi], kchunk <= qchunk)
    else:
        for j in range(n_new):
            block((klo_ref[j * tkn:(j + 1) * tkn, :], khi_ref[j * tkn:(j + 1) * tkn, :]), vt_ref[j], None)

    lv = lv_ref[...]
    lam = (jnp.exp(jnp.sum(lv[0:1] * lv[1:2], axis=1, keepdims=True))
           - jnp.exp(jnp.sum(lv[2:3] * lv[3:4], axis=1, keepdims=True)) + lam_init)
    o_t = acc_ref[0] / l_ref[0] - lam * (acc_ref[1] / l_ref[1])
    o_ref[0] = (_rms(o_t.T, sg_ref[...]) * (1.0 - lam_init)).astype(o_ref.dtype)


def _diff_attention(qa, ka, va, cache, lam_vec, subln, *, causal, hda, lam_init):
    b, s, w = qa.shape
    n_heads = w // LANE
    has_cache = cache is not None
    if causal:
        tq = tk = min(ATTN_TILE, s)
        n_new = s // tk
    else:
        tq, tk, n_new = s, ATTN_TILE, 1
    n_cache = cache[0].shape[1] // tk if has_cache else 0
    kernel = functools.partial(_attn_kernel, causal=causal, has_cache=has_cache, tq=tq, tk=tk,
                               n_new=n_new, n_cache=n_cache, hda=hda, lam_init=lam_init)
    full = lambda bi, hi, qi: (bi, 0, hi)
    in_specs = [pl.BlockSpec((1, tq, LANE), lambda bi, hi, qi: (bi, qi, hi)),
                pl.BlockSpec((1, s, LANE), full), pl.BlockSpec((1, s, LANE), full)]
    args = [qa, ka, va]
    scratch = [pltpu.VMEM((s, LANE), BF16), pltpu.VMEM((s, LANE), BF16),
               pltpu.VMEM((n_new, LANE, s // n_new), BF16)]
    if has_cache:
        p = cache[0].shape[1]
        in_specs += [pl.BlockSpec((1, p, LANE), full), pl.BlockSpec((1, p, LANE), full)]
        args += list(cache)
        scratch += [pltpu.VMEM((p, LANE), BF16), pltpu.VMEM((p, LANE), BF16),
                    pltpu.VMEM((n_cache, LANE, tk), BF16)]
    in_specs += [pl.BlockSpec(lam_vec.shape, lambda bi, hi, qi: (0, 0)),
                 pl.BlockSpec((1, LANE), lambda bi, hi, qi: (0, 0))]
    args += [lam_vec, subln]
    scratch += [pltpu.VMEM((2, 1, tq), F32), pltpu.VMEM((2, 1, tq), F32), pltpu.VMEM((2, LANE, tq), F32)]
    return pl.pallas_call(
        kernel,
        grid=(b, n_heads, s // tq),
        in_specs=in_specs,
        out_specs=pl.BlockSpec((1, tq, LANE), lambda bi, hi, qi: (bi, qi, hi)),
        out_shape=jax.ShapeDtypeStruct((b, s, w), BF16),
        scratch_shapes=scratch,
        compiler_params=_cparams("parallel", "parallel", "arbitrary"),
        name="diff_attn",
    )(*args)


def _retention_kernel(*refs, has_state, chunk):
    if has_state:
        q_ref, k_ref, v_ref, g_ref, rg_ref, s0_ref, y_ref, so_ref, st_ref = refs
    else:
        q_ref, k_ref, v_ref, g_ref, rg_ref, y_ref, so_ref, st_ref = refs
    hi, ci = pl.program_id(1), pl.program_id(2)

    @pl.when(ci == 0)
    def _init():
        st_ref[...] = s0_ref[0, 0] if has_state else jnp.zeros(st_ref.shape, F32)

    head = jnp.full((1, 1), hi, I32).astype(F32)
    log_gamma = jnp.log(1.0 - jnp.exp2(-5.0 - head))
    q, k, v = q_ref[0], k_ref[0], v_ref[0]
    row = lax.broadcasted_iota(I32, (chunk, chunk), 0)
    col = lax.broadcasted_iota(I32, (chunk, chunk), 1)
    dist = (row - col).astype(F32)
    decay = jnp.where(dist >= 0, jnp.exp(log_gamma * jnp.maximum(dist, 0.0)), 0.0)
    idx = lax.broadcasted_iota(I32, (chunk, 1), 0).astype(F32)
    xi = jnp.exp(log_gamma * (idx + 1.0))
    zeta = jnp.exp(log_gamma * (chunk - 1.0 - idx))
    inner = lax.dot_general(q, k, (((1,), (1,)), ((), ())), preferred_element_type=F32) * decay
    state = st_ref[...]
    out = (jnp.dot(inner.astype(BF16), v, preferred_element_type=F32)
           + jnp.dot(q, state.astype(BF16), preferred_element_type=F32) * xi)
    kz = (k.astype(F32) * zeta).astype(BF16)
    new_state = (jnp.exp(log_gamma * chunk) * state
                 + lax.dot_general(kz, v, (((0,), (0,)), ((), ())), preferred_element_type=F32))
    st_ref[...] = new_state
    gate = g_ref[0].astype(F32)
    y_ref[0] = (_rms(out, rg_ref[...]) * (gate * jax.nn.sigmoid(gate))).astype(y_ref.dtype)

    @pl.when(ci == pl.num_programs(2) - 1)
    def _emit_state():
        so_ref[0, 0] = new_state


def _retention(qb, kb, vb, gb, ret_g, state0):
    b, s, w = qb.shape
    n_heads = w // LANE
    chunk = min(RET_CHUNK, s)
    has_state = state0 is not None
    blk = pl.BlockSpec((1, chunk, LANE), lambda bi, hi, ci: (bi, ci, hi))
    st_spec = pl.BlockSpec((1, 1, LANE, LANE), lambda bi, hi, ci: (bi, hi, 0, 0))
    in_specs = [blk, blk, blk, blk, pl.BlockSpec((1, LANE), lambda bi, hi, ci: (0, 0))]
    args = [qb, kb, vb, gb, ret_g]
    if has_state:
        in_specs.append(st_spec)
        args.append(state0)
    return pl.pallas_call(
        functools.partial(_retention_kernel, has_state=has_state, chunk=chunk),
        grid=(b, n_heads, s // chunk),
        in_specs=in_specs,
        out_specs=[blk, st_spec],
        out_shape=[jax.ShapeDtypeStruct((b, s, w), BF16), jax.ShapeDtypeStruct((b, n_heads, LANE, LANE), F32)],
        scratch_shapes=[pltpu.VMEM((LANE, LANE), F32)],
        compiler_params=_cparams("parallel", "parallel", "arbitrary"),
        name="retention",
    )(*args)


def _router_logits(h_bf, wr_ref, br_ref):
    return jnp.dot(h_bf, wr_ref[...], preferred_element_type=F32) + br_ref[...]


def _first_index_of_max(vals, lane):
    vmax = jnp.max(vals, axis=1, keepdims=True)
    first = jnp.min(jnp.where(vals == vmax, lane.astype(F32), float(ROUTER_LANES)), axis=1, keepdims=True)
    return vmax, first.astype(I32)


def _mix_kernel(x_ref, ya_ref, yb_ref, woa_ref, wob_ref, g_ref, wr_ref, br_ref,
                x1_ref, cls_ref, cnt_ref, *, n_groups, n_experts):
    x1 = (x_ref[...] + jnp.dot(ya_ref[...], woa_ref[...], preferred_element_type=F32)
          + jnp.dot(yb_ref[...], wob_ref[...], preferred_element_type=F32))
    x1_ref[...] = x1
    logits = _router_logits(_rms(x1, g_ref[...]).astype(BF16), wr_ref, br_ref)
    lane = lax.broadcasted_iota(I32, logits.shape, 1)
    _, g_sel = _first_index_of_max(jnp.where(lane < n_groups, logits, NEG), lane)
    e_lo = n_groups + g_sel * n_experts
    e_logits = jnp.where((lane >= e_lo) & (lane < e_lo + n_experts), logits, NEG)
    _, top1 = _first_index_of_max(e_logits, lane)
    _, top2 = _first_index_of_max(jnp.where(lane == top1, NEG, e_logits), lane)
    ea = jnp.minimum(top1, top2) - e_lo
    eb = jnp.maximum(top1, top2) - e_lo
    pair = ((ea * (2 * n_experts - 1 - ea)) >> 1) + (eb - ea - 1)
    n_pairs = n_experts * (n_experts - 1) // 2
    cls = g_sel * n_pairs + pair
    onehot = (lane == cls).astype(F32)

    @pl.when(pl.program_id(0) == 0)
    def _zero():
        cnt_ref[...] = jnp.zeros(cnt_ref.shape, F32)

    cnt_ref[...] += jnp.sum(onehot, axis=0, keepdims=True)
    cls_rows = jnp.broadcast_to(cls.astype(F32), logits.shape).T
    cls_ref[0] = cls_rows[0:1].astype(I32)


def _mix(x2d, ya, yb, wo_a, wo_b, gain, wr, br, *, n_groups, n_experts):
    t, d = x2d.shape
    ts = min(TOKEN_TILE, t)
    row = lambda i: (i, 0)
    const = lambda i: (0, 0)
    return pl.pallas_call(
        functools.partial(_mix_kernel, n_groups=n_groups, n_experts=n_experts),
        grid=(t // ts,),
        in_specs=[pl.BlockSpec((ts, d), row), pl.BlockSpec((ts, ya.shape[1]), row),
                  pl.BlockSpec((ts, yb.shape[1]), row), pl.BlockSpec(wo_a.shape, const),
                  pl.BlockSpec(wo_b.shape, const), pl.BlockSpec((1, d), const),
                  pl.BlockSpec(wr.shape, const), pl.BlockSpec(br.shape, const)],
        out_specs=[pl.BlockSpec((ts, d), row), pl.BlockSpec((1, 1, ts), lambda i: (i, 0, 0)),
                   pl.BlockSpec((1, ROUTER_LANES), const)],
        out_shape=[jax.ShapeDtypeStruct((t, d), F32), jax.ShapeDtypeStruct((t // ts, 1, ts), I32),
                   jax.ShapeDtypeStruct((1, ROUTER_LANES), F32)],
        compiler_params=_cparams("arbitrary"),
        name="mix",
    )(x2d, ya, yb, wo_a, wo_b, gain, wr, br)


def _row_copy(src, src_row, dst, dst_row, sem):
    return pltpu.make_async_copy(src.at[pl.ds(src_row, 1), :], dst.at[pl.ds(dst_row, 1), :], sem)


def _dispatch_kernel(start_ref, end_ref, cls_ref, x_ref, xs_ref, pos_ref, next_ref, zero_ref, sem, *, n_cls):
    step = pl.program_id(0)
    ts = x_ref.shape[0]

    @pl.when(step == 0)
    def _init():
        zero_ref[...] = jnp.zeros(zero_ref.shape, F32)

        def seed(c, carry):
            next_ref[c] = start_ref[c]
            return carry
        lax.fori_loop(0, n_cls, seed, 0)

    def place(r, carry):
        c = cls_ref[0, 0, r]
        slot = next_ref[c]
        next_ref[c] = slot + 1
        pos_ref[0, 0, r] = slot
        _row_copy(x_ref, r, xs_ref, slot, sem).start()
        return carry
    lax.fori_loop(0, ts, place, 0)

    def drain(r, carry):
        _row_copy(x_ref, 0, xs_ref, 0, sem).wait()
        return carry
    lax.fori_loop(0, ts, drain, 0)

    @pl.when(step == pl.num_programs(0) - 1)
    def _pad():
        def per_class(c, carry):
            lo, hi = next_ref[c], end_ref[c]

            def fill(slot, inner):
                _row_copy(zero_ref, 0, xs_ref, slot, sem).start()
                return inner
            lax.fori_loop(lo, hi, fill, 0)

            def fill_wait(slot, inner):
                _row_copy(zero_ref, 0, xs_ref, 0, sem).wait()
                return inner
            lax.fori_loop(lo, hi, fill_wait, 0)
            return carry
        lax.fori_loop(0, n_cls, per_class, 0)

        tile = zero_ref.shape[0]
        first_free = lax.div(end_ref[n_cls - 1], tile)
        n_tiles = xs_ref.shape[0] // tile

        def tile_copy(ti):
            return pltpu.make_async_copy(zero_ref, xs_ref.at[pl.ds(pl.multiple_of(ti * tile, tile), tile), :], sem)

        def clear(ti, carry):
            tile_copy(ti).start()
            return carry
        lax.fori_loop(first_free, n_tiles, clear, 0)

        def clear_wait(ti, carry):
            tile_copy(ti).wait()
            return carry
        lax.fori_loop(first_free, n_tiles, clear_wait, 0)


def _dispatch(x1, cls, slot_start, slot_end, n_slots, *, n_cls):
    t, d = x1.shape
    n_tiles, _, ts = cls.shape
    smem_blk = pl.BlockSpec((1, 1, ts), lambda i, *_: (i, 0, 0), memory_space=pltpu.SMEM)
    return pl.pallas_call(
        functools.partial(_dispatch_kernel, n_cls=n_cls),
        grid_spec=pltpu.PrefetchScalarGridSpec(
            num_scalar_prefetch=2,
            grid=(n_tiles,),
            in_specs=[smem_blk, pl.BlockSpec((ts, d), lambda i, *_: (i, 0))],
            out_specs=[pl.BlockSpec(memory_space=pl.ANY), smem_blk],
            scratch_shapes=[pltpu.SMEM((ROUTER_LANES,), I32), pltpu.VMEM((EXPERT_TILE, d), F32),
                            pltpu.SemaphoreType.DMA],
        ),
        out_shape=[jax.ShapeDtypeStruct((n_slots, d), F32), jax.ShapeDtypeStruct(cls.shape, I32)],
        compiler_params=_cparams("arbitrary"),
        name="dispatch",
    )(slot_start, slot_end, cls, x1)


def _unpermute_kernel(pos_ref, ys_ref, o_ref, sem):
    ts = o_ref.shape[0]

    def fetch(r, carry):
        _row_copy(ys_ref, pos_ref[0, 0, r], o_ref, r, sem).start()
        return carry
    lax.fori_loop(0, ts, fetch, 0)

    def drain(r, carry):
        _row_copy(ys_ref, 0, o_ref, 0, sem).wait()
        return carry
    lax.fori_loop(0, ts, drain, 0)


def _unpermute(ys, pos, t):
    d = ys.shape[1]
    n_tiles, _, ts = pos.shape
    return pl.pallas_call(
        _unpermute_kernel,
        grid=(n_tiles,),
        in_specs=[pl.BlockSpec((1, 1, ts), lambda i: (i, 0, 0), memory_space=pltpu.SMEM),
                  pl.BlockSpec(memory_space=pl.ANY)],
        out_specs=pl.BlockSpec((ts, d), lambda i: (i, 0)),
        out_shape=jax.ShapeDtypeStruct((t, d), F32),
        scratch_shapes=[pltpu.SemaphoreType.DMA],
        compiler_params=_cparams("arbitrary"),
        name="unpermute",
    )(pos, ys)


def _expert_kernel(tg_ref, ta_ref, tb_ref, nv_ref, xs_ref, g_ref, wr_ref, br_ref,
                   wga_ref, wua_ref, wda_ref, wgb_ref, wub_ref, wdb_ref, ys_ref, *, n_groups, n_experts):
    i = pl.program_id(0)

    @pl.when(nv_ref[i] == 0)
    def _unused_tile():
        ys_ref[...] = jnp.zeros(ys_ref.shape, F32)

    @pl.when(nv_ref[i] > 0)
    def _run():
        x = xs_ref[...]
        h = _rms(x, g_ref[...]).astype(BF16)
        logits = _router_logits(h, wr_ref, br_ref)
        lane = lax.broadcasted_iota(I32, logits.shape, 1)
        grp, ea, eb = tg_ref[i], ta_ref[i], tb_ref[i]

        def pick(idx):
            return jnp.sum(jnp.where(lane == idx, logits, 0.0), axis=1, keepdims=True)

        g_logits = jnp.where(lane < n_groups, logits, NEG)
        g_max = jnp.max(g_logits, axis=1, keepdims=True)
        g_w = jnp.exp(pick(grp) - g_max) / jnp.sum(jnp.exp(g_logits - g_max), axis=1, keepdims=True)
        e_lo = n_groups + grp * n_experts
        w_a = jax.nn.sigmoid(pick(e_lo + ea) - pick(e_lo + eb))
        c_a, c_b = g_w * w_a, g_w * (1.0 - w_a)

        def expert(wg_ref, wu_ref, wd_ref, comb):
            gate = jnp.dot(h, wg_ref[0, 0], preferred_element_type=F32)
            up = jnp.dot(h, wu_ref[0, 0], preferred_element_type=F32)
            act = (gate * jax.nn.sigmoid(gate) * up * comb).astype(BF16)
            return jnp.dot(act, wd_ref[0, 0], preferred_element_type=F32)

        ys_ref[...] = x + expert(wga_ref, wua_ref, wda_ref, c_a) + expert(wgb_ref, wub_ref, wdb_ref, c_b)


def _experts(xs, gain, wr, br, wg, wu, wd, meta, *, n_groups, n_experts):
    n_slots, d = xs.shape
    de = wg.shape[-1]
    tile = EXPERT_TILE
    tile_g, tile_a, tile_b, tile_nv = meta
    rows = lambda i, *_: (i, 0)
    const = lambda i, *_: (0, 0)
    sel_a = lambda i, tg, ta, tb, nv: (tg[i], ta[i], 0, 0)
    sel_b = lambda i, tg, ta, tb, nv: (tg[i], tb[i], 0, 0)
    up_spec_a = pl.BlockSpec((1, 1, d, de), sel_a)
    up_spec_b = pl.BlockSpec((1, 1, d, de), sel_b)
    return pl.pallas_call(
        functools.partial(_expert_kernel, n_groups=n_groups, n_experts=n_experts),
        grid_spec=pltpu.PrefetchScalarGridSpec(
            num_scalar_prefetch=4,
            grid=(n_slots // tile,),
            in_specs=[pl.BlockSpec((tile, d), rows), pl.BlockSpec((1, d), const),
                      pl.BlockSpec(wr.shape, const), pl.BlockSpec(br.shape, const),
                      up_spec_a, up_spec_a, pl.BlockSpec((1, 1, de, d), sel_a),
                      up_spec_b, up_spec_b, pl.BlockSpec((1, 1, de, d), sel_b)],
            out_specs=pl.BlockSpec((tile, d), rows),
        ),
        out_shape=jax.ShapeDtypeStruct((n_slots, d), F32),
        compiler_params=_cparams("arbitrary"),
        name="experts",
    )(tile_g, tile_a, tile_b, tile_nv, xs, gain, wr, br, wg, wu, wd, wg, wu, wd)


def _class_layout(counts, n_cls, n_pairs, n_experts, n_tiles):
    tile = EXPERT_TILE
    counts = counts[0, :n_cls].astype(I32)
    tiles_c = (counts + tile - 1) // tile
    tile_end = jnp.cumsum(tiles_c)
    tile_start = tile_end - tiles_c
    n_active = tile_end[-1]
    pad = ROUTER_LANES - n_cls
    slot_start = jnp.pad(tile_start * tile, (0, pad))
    slot_end = jnp.pad(tile_end * tile, (0, pad))
    t_idx = jnp.arange(n_tiles, dtype=I32)
    t_cls = jnp.minimum(jnp.searchsorted(tile_end, t_idx, side="right").astype(I32), n_cls - 1)
    active = t_idx < n_active
    t_nv = jnp.where(active, jnp.clip(counts[t_cls] - (t_idx - tile_start[t_cls]) * tile, 0, tile), 0)
    t_cls = jnp.where(active, t_cls, t_cls[jnp.maximum(n_active - 1, 0)])
    pair_a, pair_b = np.triu_indices(n_experts, 1)
    pair = t_cls % n_pairs
    meta = (t_cls // n_pairs, jnp.asarray(pair_a, I32)[pair], jnp.asarray(pair_b, I32)[pair], t_nv.astype(I32))
    return slot_start.astype(I32), slot_end.astype(I32), meta


def _rope_tables(pos, group, signed_half):
    half = group // 2
    lane = np.arange(LANE)
    inv = (ROPE_THETA ** (-jnp.arange(0, group, 2, dtype=F32) / group))[lane % half]
    ang = pos.astype(F32)[:, None] * inv[None, :]
    sign = np.where((lane % group) < signed_half, -1.0, 1.0).astype(np.float32)
    return jnp.cos(ang), jnp.sin(ang) * sign[None, :]


def _layer(x, pos, cache, state0, p, li, dims):
    b, s, d = x.shape
    aw, bw, hda, dkb = dims["aw"], dims["bw"], dims["hda"], dims["dkb"]
    n_groups, n_experts = dims["n_groups"], dims["n_experts"]
    lam_init = 0.8 - 0.6 * float(np.exp(-0.3 * li))
    t = b * s
    x2d = x.reshape(t, d)
    tabs = (*_rope_tables(pos, hda, hda // 2), *_rope_tables(pos, dkb, dkb // 2))
    qa, ka, va, qb, kb, vb, gb = _project(x2d, s, p["norm_attn"], p["w_in"], p["gm"], p["gq"], p["gk"], tabs,
                                          aw=aw, bw=bw, hda=hda, dkb=dkb)
    shape3 = lambda a: a.reshape(b, s, a.shape[-1])
    ya = _diff_attention(shape3(qa), shape3(ka), shape3(va), cache, p["lam_vec"], p["subln"],
                         causal=cache is None, hda=hda, lam_init=lam_init)
    yb, new_state = _retention(shape3(qb), shape3(kb), shape3(vb), shape3(gb), p["ret_norm"], state0)
    x1, cls, counts = _mix(x2d, ya.reshape(t, aw), yb.reshape(t, bw), p["wo_a"], p["wo_b"], p["norm_ffn"],
                           p["wr"], p["br"], n_groups=n_groups, n_experts=n_experts)
    n_pairs = n_experts * (n_experts - 1) // 2
    n_cls = n_groups * n_pairs
    n_tiles = t // EXPERT_TILE + n_cls
    slot_start, slot_end, meta = _class_layout(counts, n_cls, n_pairs, n_experts, n_tiles)
    xs, slot_of = _dispatch(x1, cls, slot_start, slot_end, n_tiles * EXPERT_TILE, n_cls=n_cls)
    ys = _experts(xs, p["norm_ffn"], p["wr"], p["br"], p["w_gate"], p["w_up"], p["w_down"], meta,
                  n_groups=n_groups, n_experts=n_experts)
    x2 = _unpermute(ys, slot_of, t)
    return x2.reshape(b, s, d), ka, va, new_state


def kernel(x_prompt, x_sample, cache_k, cache_v, state_ret, norm_attn, w_in, q_norm, k_norm, lam_vec,
           subln, ret_norm, w_out, norm_ffn, w_group, b_group, w_expert, b_expert, w_gate, w_up, w_down):
    depth, dec_b, past, heads2, hda = cache_k.shape
    _, _, n_ret, dkb, dvb = state_ret.shape
    n_groups, n_experts = w_gate.shape[1], w_gate.shape[2]
    aw, bw = heads2 * hda, n_ret * dkb
    assert 2 * hda == LANE and dkb == LANE and dvb == LANE and aw % LANE == 0
    assert n_groups * (1 + n_experts) <= ROUTER_LANES
    assert n_groups * n_experts * (n_experts - 1) // 2 <= ROUTER_LANES
    dims = dict(aw=aw, bw=bw, hda=hda, dkb=dkb, n_groups=n_groups, n_experts=n_experts)
    d = x_prompt.shape[-1]
    group_of = np.arange(aw) // hda
    gm = jnp.asarray((group_of[:, None] == group_of[None, :]) / hda, BF16)
    row = lambda v: v.reshape(1, -1).astype(F32)

    def layer_params(li):
        wr = jnp.concatenate([w_group[li], w_expert[li].reshape(d, n_groups * n_experts)], axis=1)
        br = jnp.concatenate([b_group[li], b_expert[li].reshape(-1)])
        lane_pad = ROUTER_LANES - wr.shape[1]
        return dict(
            norm_attn=row(norm_attn[li]), w_in=w_in[li].astype(BF16), gm=gm,
            gq=row(jnp.tile(q_norm[li], heads2)), gk=row(jnp.tile(k_norm[li], heads2)),
            lam_vec=lam_vec[li].astype(F32), subln=row(subln[li]), ret_norm=row(ret_norm[li]),
            wo_a=w_out[li, :aw].astype(BF16), wo_b=w_out[li, aw:].astype(BF16), norm_ffn=row(norm_ffn[li]),
            wr=jnp.pad(wr, ((0, 0), (0, lane_pad))).astype(BF16), br=row(jnp.pad(br, (0, lane_pad))),
            w_gate=w_gate[li].astype(BF16), w_up=w_up[li].astype(BF16), w_down=w_down[li].astype(BF16))

    pos_p = jnp.arange(x_prompt.shape[1])
    pos_s = past + jnp.arange(x_sample.shape[1])
    xp, xs = x_prompt, x_sample
    outs = [[] for _ in range(6)]
    for li in range(depth):
        p = layer_params(li)
        xp, ka, va, st = _layer(xp, pos_p, None, None, p, li, dims)
        bp, sp = xp.shape[:2]
        outs[0].append(ka.reshape(bp, sp, heads2, hda))
        outs[1].append(va.reshape(bp, sp, heads2 // 2, 2 * hda))
        outs[2].append(st)
        cache = (cache_k[li].reshape(dec_b, past, aw), cache_v[li].reshape(dec_b, past, aw))
        xs, ka, va, st = _layer(xs, pos_s, cache, state_ret[li], p, li, dims)
        bs, ss = xs.shape[:2]
        outs[3].append(ka.reshape(bs, ss, heads2, hda))
        outs[4].append(va.reshape(bs, ss, heads2 // 2, 2 * hda))
        outs[5].append(st)
    return (xp, xs, *(jnp.stack(o) for o in outs))
```

```python
import functools

import jax
import jax.numpy as jnp
import numpy as np
from jax import lax
from jax.experimental import pallas as pl
from jax.experimental.pallas import tpu as pltpu

F32 = jnp.float32
BF16 = jnp.bfloat16
I32 = jnp.int32

CHUNK = 64
ROPE_THETA = 10000.0
EPS = 1e-6
NEG = -1e30
LANE = 128
SUBLANES = 8
ROUTER_LANES = 128
VMEM_LIMIT_BYTES = 52 * 1024 * 1024

TOKEN_TILE = 512
ATTN_TILE = 256
CACHE_TILE = 512
LOG2E = 1.4426950408889634
RET_CHUNK = 256
EXPERT_TILE = 128


def _cparams(*sem):
    return pltpu.CompilerParams(dimension_semantics=sem, vmem_limit_bytes=VMEM_LIMIT_BYTES)


def _rms(x, gain):
    ms = jnp.mean(x * x, axis=-1, keepdims=True)
    return x * lax.rsqrt(ms + EPS) * gain


def _rot_half(y, half):
    if 2 * half == LANE:
        return pltpu.roll(y, half, axis=1)
    lane = lax.broadcasted_iota(I32, y.shape, 1)
    first = (lane & (2 * half - 1)) < half
    return jnp.where(first, pltpu.roll(y, LANE - half, axis=1), pltpu.roll(y, half, axis=1))


def _row_tiled(t, d):
    assert d % (SUBLANES * LANE) == 0
    return (t, d // LANE, LANE)


def _rows_spec(ts, shape, index=lambda i, *_: i):
    zeros = (0,) * (len(shape) - 1)
    return pl.BlockSpec((ts, *shape[1:]), lambda i, *s: (index(i, *s), *zeros))


def _load_rows(ref):
    if len(ref.shape) == 2:
        return ref[...]
    return jnp.concatenate([ref[:, c, :] for c in range(ref.shape[1])], axis=1)


def _store_rows(ref, val):
    if len(ref.shape) == 2:
        ref[...] = val
    else:
        for c in range(ref.shape[1]):
            ref[:, c, :] = val[:, c * LANE:(c + 1) * LANE]


def _proj_kernel(x_ref, g_ref, w_ref, gm_ref, gq_ref, gk_ref, ca_ref, sa_ref, cb_ref, sb_ref,
                 qa_ref, ka_ref, va_ref, qb_ref, kb_ref, vb_ref, gb_ref, *, aw, bw, hda, dkb):
    h = _rms(_load_rows(x_ref), g_ref[...]).astype(BF16)

    def proj(c0, width):
        return jnp.dot(h, w_ref[:, c0:c0 + width], preferred_element_type=F32)

    def qk_norm_rot(z, gain_ref, out_ref):
        ss = jnp.dot((z * z).astype(BF16), gm_ref[...], preferred_element_type=F32)
        y = z * lax.rsqrt(ss + EPS) * gain_ref[...]
        for c in range(aw // LANE):
            yc = y[:, c * LANE:(c + 1) * LANE]
            r = yc * ca_ref[...] + _rot_half(yc, hda // 2) * sa_ref[...]
            out_ref[:, c * LANE:(c + 1) * LANE] = r.astype(out_ref.dtype)

    def rot_b(z, out_ref, scale):
        for c in range(bw // LANE):
            zc = z[:, c * LANE:(c + 1) * LANE]
            r = zc * cb_ref[...] + _rot_half(zc, dkb // 2) * sb_ref[...]
            out_ref[:, c * LANE:(c + 1) * LANE] = (r * scale).astype(out_ref.dtype)

    qk_norm_rot(proj(0, aw), gq_ref, qa_ref)
    qk_norm_rot(proj(aw, aw), gk_ref, ka_ref)
    va_ref[...] = proj(2 * aw, aw)
    rot_b(proj(3 * aw, bw), qb_ref, 1.0)
    rot_b(proj(3 * aw + bw, bw), kb_ref, dkb ** -0.5)
    vb_ref[...] = proj(3 * aw + 2 * bw, bw).astype(BF16)
    gb_ref[...] = proj(3 * aw + 3 * bw, bw).astype(BF16)


def _project(x2d, seq, gain, w_bf, gm, gq_t, gk_t, tabs, *, aw, bw, hda, dkb):
    t, d = x2d.shape[0], gain.shape[1]
    ts = min(TOKEN_TILE, seq)
    n_s = seq // ts
    row = lambda i: (i, 0)
    const = lambda i: (0, 0)
    tab = lambda i: (i % n_s, 0)
    tab_spec = pl.BlockSpec((ts, LANE), tab)
    out_a = pl.BlockSpec((ts, aw), row)
    out_b = pl.BlockSpec((ts, bw), row)
    return pl.pallas_call(
        functools.partial(_proj_kernel, aw=aw, bw=bw, hda=hda, dkb=dkb),
        grid=(t // ts,),
        in_specs=[_rows_spec(ts, x2d.shape), pl.BlockSpec((1, d), const),
                  pl.BlockSpec(w_bf.shape, const), pl.BlockSpec(gm.shape, const),
                  pl.BlockSpec((1, aw), const), pl.BlockSpec((1, aw), const),
                  tab_spec, tab_spec, tab_spec, tab_spec],
        out_specs=[out_a, out_a, out_a, out_b, out_b, out_b, out_b],
        out_shape=[jax.ShapeDtypeStruct((t, aw), BF16), jax.ShapeDtypeStruct((t, aw), F32),
                   jax.ShapeDtypeStruct((t, aw), F32), jax.ShapeDtypeStruct((t, bw), BF16),
                   jax.ShapeDtypeStruct((t, bw), BF16), jax.ShapeDtypeStruct((t, bw), BF16),
                   jax.ShapeDtypeStruct((t, bw), BF16)],
        compiler_params=_cparams("parallel"),
        name="proj",
    )(x2d, gain, w_bf, gm, gq_t, gk_t, *tabs)


def _scaled_queries(q, hda):
    return (q.astype(F32) * (hda ** -0.5 * LOG2E)).astype(BF16)


def _split_halves(k, hda):
    lo = (lax.broadcasted_iota(I32, k.shape, 1) & (LANE - 1)) < hda
    return jnp.where(lo, k, 0.0).astype(BF16), jnp.where(lo, 0.0, k).astype(BF16)


def _reset_softmax_state(m_ref, l_ref, acc_ref):
    m_ref[...] = jnp.full(m_ref.shape, NEG, F32)
    l_ref[...] = jnp.zeros(l_ref.shape, F32)
    acc_ref[...] = jnp.zeros(acc_ref.shape, F32)


def _softmax_block(qs, k_halves, vts, valid, m_ref, l_ref, acc_ref):
    scores = [lax.dot_general(kk, q, (((1,), (1,)), ((), ())), preferred_element_type=F32)
              for q, halves in zip(qs, k_halves) for kk in halves]
    probs, alphas = [], []
    for c, s in enumerate(scores):
        if valid is not None:
            s = jnp.where(valid, s, NEG)
        m_old = m_ref[c]
        m_new = jnp.maximum(m_old, jnp.max(s, axis=0, keepdims=True))
        alpha = jnp.exp2(m_old - m_new)
        p = jnp.exp2(s - m_new)
        l_ref[c] = alpha * l_ref[c] + jnp.sum(p, axis=0, keepdims=True)
        m_ref[c] = m_new
        probs.append(p.astype(BF16))
        alphas.append(alpha)
    for c, (p, alpha) in enumerate(zip(probs, alphas)):
        acc_ref[c] = alpha * acc_ref[c] + jnp.dot(vts[c // 2], p, preferred_element_type=F32)


def _finish_heads(lv_ref, sg_ref, o_ref, l_ref, acc_ref, lam_init):
    lv = lv_ref[...]
    lam = (jnp.exp(jnp.sum(lv[0:1] * lv[1:2], axis=1, keepdims=True))
           - jnp.exp(jnp.sum(lv[2:3] * lv[3:4], axis=1, keepdims=True)) + lam_init)
    for h in range(o_ref.shape[2] // LANE):
        o_t = acc_ref[2 * h] / l_ref[2 * h] - lam * (acc_ref[2 * h + 1] / l_ref[2 * h + 1])
        o_ref[0, :, h * LANE:(h + 1) * LANE] = (_rms(o_t.T, sg_ref[...]) * (1.0 - lam_init)).astype(o_ref.dtype)


def _attn_prompt_kernel(q_ref, k_ref, v_ref, lv_ref, sg_ref, o_ref,
                        qs_ref, klo_ref, khi_ref, vt_ref, m_ref, l_ref, acc_ref, *, tile, hda, lam_init):
    qi = pl.program_id(1)
    n_pairs = q_ref.shape[2] // LANE
    n_blk = k_ref.shape[1] // tile

    @pl.when(qi == 0)
    def _stage_keys():
        def stage(j, carry):
            rows = pl.ds(pl.multiple_of(j * tile, tile), tile)
            klo_ref[rows, :], khi_ref[rows, :] = _split_halves(k_ref[0, rows, :], hda)
            v = v_ref[0, rows, :]
            for h in range(n_pairs):
                vt_ref[j * n_pairs + h] = v[:, h * LANE:(h + 1) * LANE].T.astype(BF16)
            return carry
        lax.fori_loop(0, n_blk, stage, 0)

    qs_ref[...] = _scaled_queries(q_ref[0], hda)
    _reset_softmax_state(m_ref, l_ref, acc_ref)

    def key_block(kb, valid):
        rows = pl.ds(pl.multiple_of(kb * tile, tile), tile)
        cols = [slice(h * LANE, (h + 1) * LANE) for h in range(n_pairs)]
        _softmax_block([qs_ref[:, c] for c in cols], [(klo_ref[rows, c], khi_ref[rows, c]) for c in cols],
                       [vt_ref[kb * n_pairs + h] for h in range(n_pairs)], valid, m_ref, l_ref, acc_ref)

    def past(kb, carry):
        key_block(kb, None)
        return carry
    lax.fori_loop(0, qi, past, 0)
    shift = CHUNK.bit_length() - 1
    kchunk = lax.broadcasted_iota(I32, (tile, tile), 0) >> shift
    qchunk = lax.broadcasted_iota(I32, (tile, tile), 1) >> shift
    key_block(qi, kchunk <= qchunk)
    _finish_heads(lv_ref, sg_ref, o_ref, l_ref, acc_ref, lam_init)


def _attn_cached_kernel(q_ref, kc_ref, vc_ref, kn_ref, vn_ref, lv_ref, sg_ref, o_ref,
                        qs_ref, m_ref, l_ref, acc_ref, *, hda, lam_init):
    pi = pl.program_id(1)
    n_pairs = q_ref.shape[2] // LANE

    @pl.when(pi == 0)
    def _start():
        qs_ref[...] = _scaled_queries(q_ref[0], hda)
        _reset_softmax_state(m_ref, l_ref, acc_ref)

    def key_block(k_src, v_src):
        k_lo, k_hi = _split_halves(k_src[0], hda)
        v = v_src[0]
        cols = [slice(h * LANE, (h + 1) * LANE) for h in range(n_pairs)]
        _softmax_block([qs_ref[:, c] for c in cols], [(k_lo[:, c], k_hi[:, c]) for c in cols],
                       [v[:, c].T.astype(BF16) for c in cols], None, m_ref, l_ref, acc_ref)

    key_block(kc_ref, vc_ref)

    @pl.when(pi == pl.num_programs(1) - 1)
    def _finish():
        key_block(kn_ref, vn_ref)
        _finish_heads(lv_ref, sg_ref, o_ref, l_ref, acc_ref, lam_init)


def _diff_attention(qa, ka, va, cache, lam_vec, subln, *, causal, hda, lam_init):
    b, s, w = qa.shape
    n_chains = 2 * (w // LANE)
    const = lambda bi, si: (0, 0)
    whole = pl.BlockSpec((1, s, w), lambda bi, si: (bi, 0, 0))
    small = [pl.BlockSpec(lam_vec.shape, const), pl.BlockSpec((1, LANE), const)]

    def state(tq):
        return [pltpu.VMEM((tq, w), BF16), pltpu.VMEM((n_chains, 1, tq), F32),
                pltpu.VMEM((n_chains, 1, tq), F32), pltpu.VMEM((n_chains, LANE, tq), F32)]

    if causal:
        tile = min(ATTN_TILE, s)
        q_blk = pl.BlockSpec((1, tile, w), lambda bi, qi: (bi, qi, 0))
        qs, *softmax_state = state(tile)
        return pl.pallas_call(
            functools.partial(_attn_prompt_kernel, tile=tile, hda=hda, lam_init=lam_init),
            grid=(b, s // tile),
            in_specs=[q_blk, whole, whole] + small,
            out_specs=q_blk,
            out_shape=jax.ShapeDtypeStruct((b, s, w), BF16),
            scratch_shapes=[qs, pltpu.VMEM((s, w), BF16), pltpu.VMEM((s, w), BF16),
                            pltpu.VMEM((s // tile * (w // LANE), LANE, tile), BF16)] + softmax_state,
            compiler_params=_cparams("parallel", "arbitrary"),
            name="diff_attn",
        )(qa, ka, va, lam_vec, subln)
    ck, cv = cache
    tkc = min(CACHE_TILE, ck.shape[1])
    cache_blk = pl.BlockSpec((1, tkc, w), lambda bi, pi: (bi, pi, 0))
    return pl.pallas_call(
        functools.partial(_attn_cached_kernel, hda=hda, lam_init=lam_init),
        grid=(b, ck.shape[1] // tkc),
        in_specs=[whole, cache_blk, cache_blk, whole, whole] + small,
        out_specs=whole,
        out_shape=jax.ShapeDtypeStruct((b, s, w), BF16),
        scratch_shapes=state(s),
        compiler_params=_cparams("parallel", "arbitrary"),
        name="diff_attn_cached",
    )(qa, ck, cv, ka, va, lam_vec, subln)


def _retention_kernel(*refs, has_state, chunk):
    if has_state:
        q_ref, k_ref, v_ref, g_ref, rg_ref, s0_ref, y_ref, so_ref, st_ref, decay_ref = refs
    else:
        q_ref, k_ref, v_ref, g_ref, rg_ref, y_ref, so_ref, st_ref, decay_ref = refs
    ci = pl.program_id(1)
    n_heads = q_ref.shape[2] // LANE
    log_gammas = [float(np.log(1.0 - 2.0 ** (-5.0 - h))) for h in range(n_heads)]

    @pl.when(ci == 0)
    def _init():
        st_ref[...] = s0_ref[0] if has_state else jnp.zeros(st_ref.shape, F32)
        row = lax.broadcasted_iota(I32, (chunk, chunk), 0)
        col = lax.broadcasted_iota(I32, (chunk, chunk), 1)
        dist = (row - col).astype(F32)
        for h, lg in enumerate(log_gammas):
            decay_ref[h] = jnp.where(dist >= 0, jnp.exp(lg * jnp.maximum(dist, 0.0)), 0.0)

    idx = lax.broadcasted_iota(I32, (chunk, 1), 0).astype(F32)
    heads = list(enumerate(log_gammas))
    cols = [slice(h * LANE, (h + 1) * LANE) for h in range(n_heads)]
    qk = [lax.dot_general(q_ref[0, :, c], k_ref[0, :, c], (((1,), (1,)), ((), ())), preferred_element_type=F32)
          for c in cols]
    cross = [jnp.dot(q_ref[0, :, c], st_ref[h].astype(BF16), preferred_element_type=F32)
             for h, c in enumerate(cols)]
    kv = []
    for (h, lg), c in zip(heads, cols):
        kz = (k_ref[0, :, c].astype(F32) * jnp.exp(lg * (chunk - 1.0 - idx))).astype(BF16)
        kv.append(lax.dot_general(kz, v_ref[0, :, c], (((0,), (0,)), ((), ())), preferred_element_type=F32))
    intra = [jnp.dot((qk[h] * decay_ref[h]).astype(BF16), v_ref[0, :, c], preferred_element_type=F32)
             for h, c in enumerate(cols)]
    for (h, lg), c in zip(heads, cols):
        out = intra[h] + cross[h] * jnp.exp(lg * (idx + 1.0))
        st_ref[h] = float(np.exp(lg * chunk)) * st_ref[h] + kv[h]
        gate = g_ref[0, :, c].astype(F32)
        y_ref[0, :, c] = (_rms(out, rg_ref[...]) * (gate * jax.nn.sigmoid(gate))).astype(y_ref.dtype)

    @pl.when(ci == pl.num_programs(1) - 1)
    def _emit_state():
        so_ref[0] = st_ref[...]


def _retention(qb, kb, vb, gb, ret_g, state0):
    b, s, w = qb.shape
    n_heads = w // LANE
    chunk = min(RET_CHUNK, s)
    has_state = state0 is not None
    blk = pl.BlockSpec((1, chunk, w), lambda bi, ci: (bi, ci, 0))
    st_spec = pl.BlockSpec((1, n_heads, LANE, LANE), lambda bi, ci: (bi, 0, 0, 0))
    in_specs = [blk, blk, blk, blk, pl.BlockSpec((1, LANE), lambda bi, ci: (0, 0))]
    args = [qb, kb, vb, gb, ret_g]
    if has_state:
        in_specs.append(st_spec)
        args.append(state0)
    return pl.pallas_call(
        functools.partial(_retention_kernel, has_state=has_state, chunk=chunk),
        grid=(b, s // chunk),
        in_specs=in_specs,
        out_specs=[blk, st_spec],
        out_shape=[jax.ShapeDtypeStruct((b, s, w), BF16), jax.ShapeDtypeStruct((b, n_heads, LANE, LANE), F32)],
        scratch_shapes=[pltpu.VMEM((n_heads, LANE, LANE), F32), pltpu.VMEM((n_heads, chunk, chunk), F32)],
        compiler_params=_cparams("parallel", "arbitrary"),
        name="retention",
    )(*args)


def _router_logits(h_bf, wr_ref, br_ref):
    return jnp.dot(h_bf, wr_ref[...], preferred_element_type=F32) + br_ref[...]


def _first_index_of_max(vals, lane):
    vmax = jnp.max(vals, axis=1, keepdims=True)
    first = jnp.min(jnp.where(vals == vmax, lane.astype(F32), float(ROUTER_LANES)), axis=1, keepdims=True)
    return vmax, first.astype(I32)


def _mix_kernel(x_ref, ya_ref, yb_ref, woa_ref, wob_ref, g_ref, wr_ref, br_ref,
                x1_ref, cls_ref, cnt_ref, *, n_groups, n_experts):
    x1 = (_load_rows(x_ref) + jnp.dot(ya_ref[...], woa_ref[...], preferred_element_type=F32)
          + jnp.dot(yb_ref[...], wob_ref[...], preferred_element_type=F32))
    _store_rows(x1_ref, x1)
    logits = _router_logits(_rms(x1, g_ref[...]).astype(BF16), wr_ref, br_ref)
    lane = lax.broadcasted_iota(I32, logits.shape, 1)
    _, g_sel = _first_index_of_max(jnp.where(lane < n_groups, logits, NEG), lane)
    e_lo = n_groups + g_sel * n_experts
    e_logits = jnp.where((lane >= e_lo) & (lane < e_lo + n_experts), logits, NEG)
    _, top1 = _first_index_of_max(e_logits, lane)
    _, top2 = _first_index_of_max(jnp.where(lane == top1, NEG, e_logits), lane)
    ea = jnp.minimum(top1, top2) - e_lo
    eb = jnp.maximum(top1, top2) - e_lo
    pair = ((ea * (2 * n_experts - 1 - ea)) >> 1) + (eb - ea - 1)
    n_pairs = n_experts * (n_experts - 1) // 2
    cls = g_sel * n_pairs + pair
    onehot = (lane == cls).astype(F32)

    @pl.when(pl.program_id(0) == 0)
    def _zero():
        cnt_ref[...] = jnp.zeros(cnt_ref.shape, F32)

    cnt_ref[...] += jnp.sum(onehot, axis=0, keepdims=True)
    cls_rows = jnp.broadcast_to(cls.astype(F32), logits.shape).T
    cls_ref[0] = cls_rows[0:1].astype(I32)


def _mix(x2d, ya, yb, wo_a, wo_b, gain, wr, br, *, n_groups, n_experts):
    t, d = x2d.shape[0], gain.shape[1]
    ts = min(TOKEN_TILE, t)
    row = lambda i: (i, 0)
    const = lambda i: (0, 0)
    x1_shape = _row_tiled(t, d)
    return pl.pallas_call(
        functools.partial(_mix_kernel, n_groups=n_groups, n_experts=n_experts),
        grid=(t // ts,),
        in_specs=[_rows_spec(ts, x2d.shape), pl.BlockSpec((ts, ya.shape[1]), row),
                  pl.BlockSpec((ts, yb.shape[1]), row), pl.BlockSpec(wo_a.shape, const),
                  pl.BlockSpec(wo_b.shape, const), pl.BlockSpec((1, d), const),
                  pl.BlockSpec(wr.shape, const), pl.BlockSpec(br.shape, const)],
        out_specs=[_rows_spec(ts, x1_shape), pl.BlockSpec((1, 1, ts), lambda i: (i, 0, 0)),
                   pl.BlockSpec((1, ROUTER_LANES), const)],
        out_shape=[jax.ShapeDtypeStruct(x1_shape, F32), jax.ShapeDtypeStruct((t // ts, 1, ts), I32),
                   jax.ShapeDtypeStruct((1, ROUTER_LANES), F32)],
        compiler_params=_cparams("arbitrary"),
        name="mix",
    )(x2d, ya, yb, wo_a, wo_b, gain, wr, br)


ISSUE_UNROLL = 8


def _row_copy(src, src_row, dst, dst_row, sem):
    return pltpu.make_async_copy(src.at[src_row], dst.at[dst_row], sem)


def _rows_copy(src, dst, n, sem):
    return pltpu.make_async_copy(src.at[pl.ds(0, n)], dst.at[pl.ds(0, n)], sem)


def _dispatch_kernel(start_ref, end_ref, cls_ref, x_ref, xs_ref, pos_ref, next_ref, zero_ref, sem, *, n_cls):
    step = pl.program_id(0)
    ts = x_ref.shape[0]

    @pl.when(step == 0)
    def _init():
        zero_ref[...] = jnp.zeros(zero_ref.shape, F32)

        def seed(c, carry):
            next_ref[c] = start_ref[c]
            return carry
        lax.fori_loop(0, n_cls, seed, 0)

    def place(r, carry):
        c = cls_ref[0, 0, r]
        slot = next_ref[c]
        next_ref[c] = slot + 1
        pos_ref[0, 0, r] = slot
        _row_copy(x_ref, r, xs_ref, slot, sem).start()
        return carry
    lax.fori_loop(0, ts, place, 0, unroll=ISSUE_UNROLL)
    _rows_copy(x_ref, xs_ref, ts, sem).wait()

    @pl.when(step == pl.num_programs(0) - 1)
    def _pad():
        def per_class(c, carry):
            lo, hi = next_ref[c], end_ref[c]

            def fill(slot, inner):
                _row_copy(zero_ref, 0, xs_ref, slot, sem).start()
                return inner
            lax.fori_loop(lo, hi, fill, 0)

            def fill_wait(slot, inner):
                _row_copy(zero_ref, 0, xs_ref, 0, sem).wait()
                return inner
            lax.fori_loop(lo, hi, fill_wait, 0)
            return carry
        lax.fori_loop(0, n_cls, per_class, 0)

        tile = zero_ref.shape[0]
        first_free = lax.div(end_ref[n_cls - 1], tile)
        n_tiles = xs_ref.shape[0] // tile

        def tile_copy(ti):
            return pltpu.make_async_copy(zero_ref, xs_ref.at[pl.ds(pl.multiple_of(ti * tile, tile), tile)], sem)

        def clear(ti, carry):
            tile_copy(ti).start()
            return carry
        lax.fori_loop(first_free, n_tiles, clear, 0)

        def clear_wait(ti, carry):
            tile_copy(ti).wait()
            return carry
        lax.fori_loop(first_free, n_tiles, clear_wait, 0)


def _dispatch(x1, cls, slot_start, slot_end, n_slots, *, n_cls):
    n_tiles, _, ts = cls.shape
    smem_blk = pl.BlockSpec((1, 1, ts), lambda i, *_: (i, 0, 0), memory_space=pltpu.SMEM)
    return pl.pallas_call(
        functools.partial(_dispatch_kernel, n_cls=n_cls),
        grid_spec=pltpu.PrefetchScalarGridSpec(
            num_scalar_prefetch=2,
            grid=(n_tiles,),
            in_specs=[smem_blk, _rows_spec(ts, x1.shape)],
            out_specs=[pl.BlockSpec(memory_space=pl.ANY), smem_blk],
            scratch_shapes=[pltpu.SMEM((ROUTER_LANES,), I32), pltpu.VMEM((EXPERT_TILE, *x1.shape[1:]), F32),
                            pltpu.SemaphoreType.DMA],
        ),
        out_shape=[jax.ShapeDtypeStruct((n_slots, *x1.shape[1:]), F32), jax.ShapeDtypeStruct(cls.shape, I32)],
        compiler_params=_cparams("arbitrary"),
        name="dispatch",
    )(slot_start, slot_end, cls, x1)


def _unpermute_kernel(pos_ref, ys_ref, o_ref, *scratch):
    *stage, sem = scratch
    rows_ref = stage[0] if stage else o_ref
    ts = o_ref.shape[0]

    def fetch(r, carry):
        _row_copy(ys_ref, pos_ref[0, 0, r], rows_ref, r, sem).start()
        return carry
    lax.fori_loop(0, ts, fetch, 0, unroll=ISSUE_UNROLL)
    _rows_copy(ys_ref, rows_ref, ts, sem).wait()
    if stage:
        o_ref[...] = _load_rows(rows_ref)


def _unpermute(ys, pos, t, *, flat):
    n_tiles, _, ts = pos.shape
    out_shape = (t, ys.shape[1] * ys.shape[2]) if flat else (t, *ys.shape[1:])
    stage = [pltpu.VMEM((ts, *ys.shape[1:]), F32)] if flat else []
    return pl.pallas_call(
        _unpermute_kernel,
        grid=(n_tiles,),
        in_specs=[pl.BlockSpec((1, 1, ts), lambda i: (i, 0, 0), memory_space=pltpu.SMEM),
                  pl.BlockSpec(memory_space=pl.ANY)],
        out_specs=_rows_spec(ts, out_shape),
        out_shape=jax.ShapeDtypeStruct(out_shape, F32),
        scratch_shapes=stage + [pltpu.SemaphoreType.DMA],
        compiler_params=_cparams("arbitrary"),
        name="unpermute",
    )(pos, ys)


def _expert_kernel(tg_ref, ta_ref, tb_ref, nv_ref, xs_ref, g_ref, wr_ref, br_ref,
                   wga_ref, wua_ref, wda_ref, wgb_ref, wub_ref, wdb_ref, ys_ref, *, n_groups, n_experts):
    i = pl.program_id(0)

    @pl.when(nv_ref[i] == 0)
    def _unused_tile():
        ys_ref[...] = jnp.zeros(ys_ref.shape, F32)

    @pl.when(nv_ref[i] > 0)
    def _run():
        x = _load_rows(xs_ref)
        h = _rms(x, g_ref[...]).astype(BF16)
        logits = _router_logits(h, wr_ref, br_ref)
        lane = lax.broadcasted_iota(I32, logits.shape, 1)
        grp, ea, eb = tg_ref[i], ta_ref[i], tb_ref[i]

        def pick(idx):
            return jnp.sum(jnp.where(lane == idx, logits, 0.0), axis=1, keepdims=True)

        g_logits = jnp.where(lane < n_groups, logits, NEG)
        g_max = jnp.max(g_logits, axis=1, keepdims=True)
        g_w = jnp.exp(pick(grp) - g_max) / jnp.sum(jnp.exp(g_logits - g_max), axis=1, keepdims=True)
        e_lo = n_groups + grp * n_experts
        w_a = jax.nn.sigmoid(pick(e_lo + ea) - pick(e_lo + eb))
        c_a, c_b = g_w * w_a, g_w * (1.0 - w_a)

        def expert(wg_ref, wu_ref, wd_ref, comb):
            gate = jnp.dot(h, wg_ref[0, 0], preferred_element_type=F32)
            up = jnp.dot(h, wu_ref[0, 0], preferred_element_type=F32)
            act = (gate * jax.nn.sigmoid(gate) * up * comb).astype(BF16)
            return jnp.dot(act, wd_ref[0, 0], preferred_element_type=F32)

        _store_rows(ys_ref, x + expert(wga_ref, wua_ref, wda_ref, c_a) + expert(wgb_ref, wub_ref, wdb_ref, c_b))


def _experts(xs, gain, wr, br, wg, wu, wd, meta, *, n_groups, n_experts):
    n_slots, d = xs.shape[0], gain.shape[1]
    de = wg.shape[-1]
    tile = EXPERT_TILE
    tile_g, tile_a, tile_b, tile_nv = meta
    const = lambda i, *_: (0, 0)
    sel_a = lambda i, tg, ta, tb, nv: (tg[i], ta[i], 0, 0)
    sel_b = lambda i, tg, ta, tb, nv: (tg[i], tb[i], 0, 0)
    up_spec_a = pl.BlockSpec((1, 1, d, de), sel_a)
    up_spec_b = pl.BlockSpec((1, 1, d, de), sel_b)
    return pl.pallas_call(
        functools.partial(_expert_kernel, n_groups=n_groups, n_experts=n_experts),
        grid_spec=pltpu.PrefetchScalarGridSpec(
            num_scalar_prefetch=4,
            grid=(n_slots // tile,),
            in_specs=[_rows_spec(tile, xs.shape), pl.BlockSpec((1, d), const),
                      pl.BlockSpec(wr.shape, const), pl.BlockSpec(br.shape, const),
                      up_spec_a, up_spec_a, pl.BlockSpec((1, 1, de, d), sel_a),
                      up_spec_b, up_spec_b, pl.BlockSpec((1, 1, de, d), sel_b)],
            out_specs=_rows_spec(tile, xs.shape),
        ),
        out_shape=jax.ShapeDtypeStruct(xs.shape, F32),
        compiler_params=_cparams("arbitrary"),
        name="experts",
    )(tile_g, tile_a, tile_b, tile_nv, xs, gain, wr, br, wg, wu, wd, wg, wu, wd)


def _class_layout(counts, n_cls, n_pairs, n_experts, n_tiles):
    tile = EXPERT_TILE
    counts = counts[0, :n_cls].astype(I32)
    tiles_c = (counts + tile - 1) // tile
    tile_end = jnp.cumsum(tiles_c)
    tile_start = tile_end - tiles_c
    n_active = tile_end[-1]
    pad = ROUTER_LANES - n_cls
    slot_start = jnp.pad(tile_start * tile, (0, pad))
    slot_end = jnp.pad(tile_end * tile, (0, pad))
    t_idx = jnp.arange(n_tiles, dtype=I32)
    t_cls = jnp.minimum(jnp.sum((tile_end[None, :] <= t_idx[:, None]).astype(I32), axis=1), n_cls - 1)
    active = t_idx < n_active
    t_nv = jnp.where(active, jnp.clip(counts[t_cls] - (t_idx - tile_start[t_cls]) * tile, 0, tile), 0)
    t_cls = jnp.where(active, t_cls, t_cls[jnp.maximum(n_active - 1, 0)])
    pair_a, pair_b = np.triu_indices(n_experts, 1)
    pair = t_cls % n_pairs
    meta = (t_cls // n_pairs, jnp.asarray(pair_a, I32)[pair], jnp.asarray(pair_b, I32)[pair], t_nv.astype(I32))
    return slot_start.astype(I32), slot_end.astype(I32), meta


def _rope_tables(pos, group, signed_half):
    half = group // 2
    lane = np.arange(LANE)
    inv = (ROPE_THETA ** (-jnp.arange(0, group, 2, dtype=F32) / group))[lane % half]
    ang = pos.astype(F32)[:, None] * inv[None, :]
    sign = np.where((lane % group) < signed_half, -1.0, 1.0).astype(np.float32)
    return jnp.cos(ang), jnp.sin(ang) * sign[None, :]


def _layer(x2d, b, s, pos, cache, state0, p, li, dims, *, last):
    aw, bw, hda, dkb = dims["aw"], dims["bw"], dims["hda"], dims["dkb"]
    n_groups, n_experts = dims["n_groups"], dims["n_experts"]
    lam_init = 0.8 - 0.6 * float(np.exp(-0.3 * li))
    t = b * s
    tabs = (*_rope_tables(pos, hda, hda // 2), *_rope_tables(pos, dkb, dkb // 2))
    qa, ka, va, qb, kb, vb, gb = _project(x2d, s, p["norm_attn"], p["w_in"], p["gm"], p["gq"], p["gk"], tabs,
                                          aw=aw, bw=bw, hda=hda, dkb=dkb)
    shape3 = lambda a: a.reshape(b, s, a.shape[-1])
    ya = _diff_attention(shape3(qa), shape3(ka), shape3(va), cache, p["lam_vec"], p["subln"],
                         causal=cache is None, hda=hda, lam_init=lam_init)
    yb, new_state = _retention(shape3(qb), shape3(kb), shape3(vb), shape3(gb), p["ret_norm"], state0)
    x1, cls, counts = _mix(x2d, ya.reshape(t, aw), yb.reshape(t, bw), p["wo_a"], p["wo_b"], p["norm_ffn"],
                           p["wr"], p["br"], n_groups=n_groups, n_experts=n_experts)
    n_pairs = n_experts * (n_experts - 1) // 2
    n_cls = n_groups * n_pairs
    n_tiles = t // EXPERT_TILE + n_cls
    slot_start, slot_end, meta = _class_layout(counts, n_cls, n_pairs, n_experts, n_tiles)
    xs, slot_of = _dispatch(x1, cls, slot_start, slot_end, n_tiles * EXPERT_TILE, n_cls=n_cls)
    ys = _experts(xs, p["norm_ffn"], p["wr"], p["br"], p["w_gate"], p["w_up"], p["w_down"], meta,
                  n_groups=n_groups, n_experts=n_experts)
    return _unpermute(ys, slot_of, t, flat=last), ka, va, new_state


def kernel(x_prompt, x_sample, cache_k, cache_v, state_ret, norm_attn, w_in, q_norm, k_norm, lam_vec,
           subln, ret_norm, w_out, norm_ffn, w_group, b_group, w_expert, b_expert, w_gate, w_up, w_down):
    depth, dec_b, past, heads2, hda = cache_k.shape
    _, _, n_ret, dkb, dvb = state_ret.shape
    n_groups, n_experts = w_gate.shape[1], w_gate.shape[2]
    aw, bw = heads2 * hda, n_ret * dkb
    assert 2 * hda == LANE and dkb == LANE and dvb == LANE and aw % LANE == 0
    assert n_groups * (1 + n_experts) <= ROUTER_LANES
    assert n_groups * n_experts * (n_experts - 1) // 2 <= ROUTER_LANES
    dims = dict(aw=aw, bw=bw, hda=hda, dkb=dkb, n_groups=n_groups, n_experts=n_experts)
    d = x_prompt.shape[-1]
    group_of = np.arange(aw) // hda
    gm = jnp.asarray((group_of[:, None] == group_of[None, :]) / hda, BF16)
    row = lambda v: v.reshape(1, -1).astype(F32)

    def layer_params(li):
        wr = jnp.concatenate([w_group[li], w_expert[li].reshape(d, n_groups * n_experts)], axis=1)
        br = jnp.concatenate([b_group[li], b_expert[li].reshape(-1)])
        lane_pad = ROUTER_LANES - wr.shape[1]
        return dict(
            norm_attn=row(norm_attn[li]), w_in=w_in[li].astype(BF16), gm=gm,
            gq=row(jnp.tile(q_norm[li], heads2)), gk=row(jnp.tile(k_norm[li], heads2)),
            lam_vec=lam_vec[li].astype(F32), subln=row(subln[li]), ret_norm=row(ret_norm[li]),
            wo_a=w_out[li, :aw].astype(BF16), wo_b=w_out[li, aw:].astype(BF16), norm_ffn=row(norm_ffn[li]),
            wr=jnp.pad(wr, ((0, 0), (0, lane_pad))).astype(BF16), br=row(jnp.pad(br, (0, lane_pad))),
            w_gate=w_gate[li].astype(BF16), w_up=w_up[li].astype(BF16), w_down=w_down[li].astype(BF16))

    pos_p = jnp.arange(x_prompt.shape[1])
    pos_s = past + jnp.arange(x_sample.shape[1])
    (bp, sp), (bs, ss) = x_prompt.shape[:2], x_sample.shape[:2]
    xp, xs = x_prompt.reshape(bp * sp, d), x_sample.reshape(bs * ss, d)
    outs = [[] for _ in range(6)]
    for li in range(depth):
        p = layer_params(li)
        last = li == depth - 1
        xp, ka, va, st = _layer(xp, bp, sp, pos_p, None, None, p, li, dims, last=last)
        outs[0].append(ka.reshape(bp, sp, heads2, hda))
        outs[1].append(va.reshape(bp, sp, heads2 // 2, 2 * hda))
        outs[2].append(st)
        cache = (cache_k[li].reshape(dec_b, past, aw), cache_v[li].reshape(dec_b, past, aw))
        xs, ka, va, st = _layer(xs, bs, ss, pos_s, cache, state_ret[li], p, li, dims, last=last)
        outs[3].append(ka.reshape(bs, ss, heads2, hda))
        outs[4].append(va.reshape(bs, ss, heads2 // 2, 2 * hda))
        outs[5].append(st)
    return (xp.reshape(bp, sp, d), xs.reshape(bs, ss, d), *(jnp.stack(o) for o in outs))
```

```python
import functools

import jax
import jax.numpy as jnp
import numpy as np
from jax import lax
from jax.experimental import pallas as pl
from jax.experimental.pallas import tpu as pltpu

F32 = jnp.float32
BF16 = jnp.bfloat16
I32 = jnp.int32

CHUNK = 64
ROPE_THETA = 10000.0
EPS = 1e-6
NEG = -1e30
LANE = 128
SUBLANES = 8
ROUTER_LANES = 128
VMEM_LIMIT_BYTES = 52 * 1024 * 1024

TOKEN_TILE = 512
ATTN_TILE = 256
CACHE_TILE = 512
LOG2E = 1.4426950408889634
RET_CHUNK = 256
EXPERT_TILE = 128
SMALL_EXPERT_TILE = 16


def _cparams(*sem):
    return pltpu.CompilerParams(dimension_semantics=sem, vmem_limit_bytes=VMEM_LIMIT_BYTES)


def _rms(x, gain):
    ms = jnp.mean(x * x, axis=-1, keepdims=True)
    return x * lax.rsqrt(ms + EPS) * gain


def _rot_half(y, half):
    if 2 * half == LANE:
        return pltpu.roll(y, half, axis=1)
    lane = lax.broadcasted_iota(I32, y.shape, 1)
    first = (lane & (2 * half - 1)) < half
    return jnp.where(first, pltpu.roll(y, LANE - half, axis=1), pltpu.roll(y, half, axis=1))


def _row_tiled(t, d):
    assert d % (SUBLANES * LANE) == 0
    return (t, d // LANE, LANE)


def _rows_spec(ts, shape):
    zeros = (0,) * (len(shape) - 1)
    return pl.BlockSpec((ts, *shape[1:]), lambda i, *s: (i, *zeros))


def _retile_copies(buf_ref, slot, out_ref, row0, sem_ref):
    rows = buf_ref.shape[1]
    copies = []
    for c in range(out_ref.shape[1] if len(out_ref.shape) == 3 else buf_ref.shape[2]):
        lanes = pl.ds(c * LANE, LANE)
        if len(out_ref.shape) == 3:
            src, dst = buf_ref.at[slot, :, lanes], out_ref.at[pl.ds(row0, rows), c, :]
        else:
            src, dst = buf_ref.at[slot, :, c, :], out_ref.at[pl.ds(row0, rows), lanes]
        copies.append(pltpu.make_async_copy(src, dst, sem_ref.at[slot]))
    return copies


def _fetch_rows(src_ref, buf_ref, sem_ref, n_steps):
    step = pl.program_id(0)
    slot = lax.rem(step, 2)
    rows = buf_ref.shape[1]

    def copies(s, row0):
        return [pltpu.make_async_copy(src_ref.at[pl.ds(row0, rows), c, :],
                                      buf_ref.at[s, :, pl.ds(c * LANE, LANE)], sem_ref.at[s])
                for c in range(src_ref.shape[1])]

    @pl.when(step == 0)
    def _first():
        for cp in copies(0, 0):
            cp.start()

    @pl.when(step + 1 < n_steps)
    def _next():
        for cp in copies(1 - slot, pl.multiple_of((step + 1) * rows, rows)):
            cp.start()

    for cp in copies(slot, 0):
        cp.wait()
    return slot


def _emit_rows(buf_ref, out_ref, sem_ref, n_steps, fill):
    step = pl.program_id(0)
    slot = lax.rem(step, 2)
    rows = buf_ref.shape[1]

    def drain(s):
        for cp in _retile_copies(buf_ref, s, out_ref, 0, sem_ref):
            cp.wait()

    @pl.when(step >= 2)
    def _reuse():
        drain(slot)

    fill(slot)
    for cp in _retile_copies(buf_ref, slot, out_ref, pl.multiple_of(step * rows, rows), sem_ref):
        cp.start()

    @pl.when(step == n_steps - 1)
    def _finish():
        drain(slot)
        if n_steps >= 2:
            drain(1 - slot)


def _proj_kernel(x_ref, g_ref, w_ref, gm_ref, gq_ref, gk_ref, ca_ref, sa_ref, cb_ref, sb_ref,
                 qa_ref, ka_ref, va_ref, qb_ref, kb_ref, vb_ref, gb_ref, *, aw, bw, hda, dkb):
    h = _rms(x_ref[...], g_ref[...]).astype(BF16)

    def proj(c0, width):
        return jnp.dot(h, w_ref[:, c0:c0 + width], preferred_element_type=F32)

    def qk_norm_rot(z, gain_ref, out_ref):
        ss = jnp.dot((z * z).astype(BF16), gm_ref[...], preferred_element_type=F32)
        y = z * lax.rsqrt(ss + EPS) * gain_ref[...]
        for c in range(aw // LANE):
            yc = y[:, c * LANE:(c + 1) * LANE]
            r = yc * ca_ref[...] + _rot_half(yc, hda // 2) * sa_ref[...]
            out_ref[:, c * LANE:(c + 1) * LANE] = r.astype(out_ref.dtype)

    def rot_b(z, out_ref, scale):
        for c in range(bw // LANE):
            zc = z[:, c * LANE:(c + 1) * LANE]
            r = zc * cb_ref[...] + _rot_half(zc, dkb // 2) * sb_ref[...]
            out_ref[:, c * LANE:(c + 1) * LANE] = (r * scale).astype(out_ref.dtype)

    qk_norm_rot(proj(0, aw), gq_ref, qa_ref)
    qk_norm_rot(proj(aw, aw), gk_ref, ka_ref)
    va_ref[...] = proj(2 * aw, aw)
    rot_b(proj(3 * aw, bw), qb_ref, 1.0)
    rot_b(proj(3 * aw + bw, bw), kb_ref, dkb ** -0.5)
    vb_ref[...] = proj(3 * aw + 2 * bw, bw).astype(BF16)
    gb_ref[...] = proj(3 * aw + 3 * bw, bw).astype(BF16)


def _project(x2d, seq, gain, w_bf, gm, gq_t, gk_t, tabs, *, aw, bw, hda, dkb):
    t, d = x2d.shape[0], gain.shape[1]
    ts = min(TOKEN_TILE, seq)
    n_s = seq // ts
    row = lambda i: (i, 0)
    const = lambda i: (0, 0)
    tab = lambda i: (i % n_s, 0)
    tab_spec = pl.BlockSpec((ts, LANE), tab)
    out_a = pl.BlockSpec((ts, aw), row)
    out_b = pl.BlockSpec((ts, bw), row)
    return pl.pallas_call(
        functools.partial(_proj_kernel, aw=aw, bw=bw, hda=hda, dkb=dkb),
        grid=(t // ts,),
        in_specs=[pl.BlockSpec((ts, d), row), pl.BlockSpec((1, d), const),
                  pl.BlockSpec(w_bf.shape, const), pl.BlockSpec(gm.shape, const),
                  pl.BlockSpec((1, aw), const), pl.BlockSpec((1, aw), const),
                  tab_spec, tab_spec, tab_spec, tab_spec],
        out_specs=[out_a, out_a, out_a, out_b, out_b, out_b, out_b],
        out_shape=[jax.ShapeDtypeStruct((t, aw), BF16), jax.ShapeDtypeStruct((t, aw), F32),
                   jax.ShapeDtypeStruct((t, aw), F32), jax.ShapeDtypeStruct((t, bw), BF16),
                   jax.ShapeDtypeStruct((t, bw), BF16), jax.ShapeDtypeStruct((t, bw), BF16),
                   jax.ShapeDtypeStruct((t, bw), BF16)],
        compiler_params=_cparams("parallel"),
        name="proj",
    )(x2d, gain, w_bf, gm, gq_t, gk_t, *tabs)


def _scaled_queries(q, hda):
    return (q.astype(F32) * (hda ** -0.5 * LOG2E)).astype(BF16)


def _split_halves(k, hda):
    lo = (lax.broadcasted_iota(I32, k.shape, 1) & (LANE - 1)) < hda
    return jnp.where(lo, k, 0.0).astype(BF16), jnp.where(lo, 0.0, k).astype(BF16)


def _reset_softmax_state(m_ref, l_ref, acc_ref):
    m_ref[...] = jnp.full(m_ref.shape, NEG, F32)
    l_ref[...] = jnp.zeros(l_ref.shape, F32)
    acc_ref[...] = jnp.zeros(acc_ref.shape, F32)


def _softmax_block(qs, k_halves, vts, valid, m_ref, l_ref, acc_ref):
    scores = [lax.dot_general(kk, q, (((1,), (1,)), ((), ())), preferred_element_type=F32)
              for q, halves in zip(qs, k_halves) for kk in halves]
    probs, alphas = [], []
    for c, s in enumerate(scores):
        if valid is not None:
            s = jnp.where(valid, s, NEG)
        m_old = m_ref[c]
        m_new = jnp.maximum(m_old, jnp.max(s, axis=0, keepdims=True))
        alpha = jnp.exp2(m_old - m_new)
        p = jnp.exp2(s - m_new)
        l_ref[c] = alpha * l_ref[c] + jnp.sum(p, axis=0, keepdims=True)
        m_ref[c] = m_new
        probs.append(p.astype(BF16))
        alphas.append(alpha)
    for c, (p, alpha) in enumerate(zip(probs, alphas)):
        acc_ref[c] = alpha * acc_ref[c] + jnp.dot(vts[c // 2], p, preferred_element_type=F32)


def _finish_heads(lv_ref, sg_ref, o_ref, l_ref, acc_ref, lam_init):
    lv = lv_ref[...]
    lam = (jnp.exp(jnp.sum(lv[0:1] * lv[1:2], axis=1, keepdims=True))
           - jnp.exp(jnp.sum(lv[2:3] * lv[3:4], axis=1, keepdims=True)) + lam_init)
    for h in range(o_ref.shape[2] // LANE):
        o_t = acc_ref[2 * h] / l_ref[2 * h] - lam * (acc_ref[2 * h + 1] / l_ref[2 * h + 1])
        o_ref[0, :, h * LANE:(h + 1) * LANE] = (_rms(o_t.T, sg_ref[...]) * (1.0 - lam_init)).astype(o_ref.dtype)


def _attn_prompt_kernel(q_ref, k_ref, v_ref, lv_ref, sg_ref, o_ref,
                        qs_ref, klo_ref, khi_ref, vt_ref, m_ref, l_ref, acc_ref, *, tile, hda, lam_init):
    qi = pl.program_id(1)
    n_pairs = q_ref.shape[2] // LANE
    n_blk = k_ref.shape[1] // tile

    @pl.when(qi == 0)
    def _stage_keys():
        def stage(j, carry):
            rows = pl.ds(pl.multiple_of(j * tile, tile), tile)
            klo_ref[rows, :], khi_ref[rows, :] = _split_halves(k_ref[0, rows, :], hda)
            v = v_ref[0, rows, :]
            for h in range(n_pairs):
                vt_ref[j * n_pairs + h] = v[:, h * LANE:(h + 1) * LANE].T.astype(BF16)
            return carry
        lax.fori_loop(0, n_blk, stage, 0)

    qs_ref[...] = _scaled_queries(q_ref[0], hda)
    _reset_softmax_state(m_ref, l_ref, acc_ref)

    def key_block(kb, valid):
        rows = pl.ds(pl.multiple_of(kb * tile, tile), tile)
        cols = [slice(h * LANE, (h + 1) * LANE) for h in range(n_pairs)]
        _softmax_block([qs_ref[:, c] for c in cols], [(klo_ref[rows, c], khi_ref[rows, c]) for c in cols],
                       [vt_ref[kb * n_pairs + h] for h in range(n_pairs)], valid, m_ref, l_ref, acc_ref)

    def past(kb, carry):
        key_block(kb, None)
        return carry
    lax.fori_loop(0, qi, past, 0)
    shift = CHUNK.bit_length() - 1
    kchunk = lax.broadcasted_iota(I32, (tile, tile), 0) >> shift
    qchunk = lax.broadcasted_iota(I32, (tile, tile), 1) >> shift
    key_block(qi, kchunk <= qchunk)
    _finish_heads(lv_ref, sg_ref, o_ref, l_ref, acc_ref, lam_init)


def _attn_cached_kernel(q_ref, kc_ref, vc_ref, kn_ref, vn_ref, lv_ref, sg_ref, o_ref,
                        qs_ref, m_ref, l_ref, acc_ref, *, hda, lam_init):
    pi = pl.program_id(1)
    n_pairs = q_ref.shape[2] // LANE

    @pl.when(pi == 0)
    def _start():
        qs_ref[...] = _scaled_queries(q_ref[0], hda)
        _reset_softmax_state(m_ref, l_ref, acc_ref)

    def key_block(k_src, v_src):
        k_lo, k_hi = _split_halves(k_src[0], hda)
        v = v_src[0]
        cols = [slice(h * LANE, (h + 1) * LANE) for h in range(n_pairs)]
        _softmax_block([qs_ref[:, c] for c in cols], [(k_lo[:, c], k_hi[:, c]) for c in cols],
                       [v[:, c].T.astype(BF16) for c in cols], None, m_ref, l_ref, acc_ref)

    key_block(kc_ref, vc_ref)

    @pl.when(pi == pl.num_programs(1) - 1)
    def _finish():
        key_block(kn_ref, vn_ref)
        _finish_heads(lv_ref, sg_ref, o_ref, l_ref, acc_ref, lam_init)


def _diff_attention(qa, ka, va, cache, lam_vec, subln, *, causal, hda, lam_init):
    b, s, w = qa.shape
    n_chains = 2 * (w // LANE)
    const = lambda bi, si: (0, 0)
    whole = pl.BlockSpec((1, s, w), lambda bi, si: (bi, 0, 0))
    small = [pl.BlockSpec(lam_vec.shape, const), pl.BlockSpec((1, LANE), const)]

    def state(tq):
        return [pltpu.VMEM((tq, w), BF16), pltpu.VMEM((n_chains, 1, tq), F32),
                pltpu.VMEM((n_chains, 1, tq), F32), pltpu.VMEM((n_chains, LANE, tq), F32)]

    if causal:
        tile = min(ATTN_TILE, s)
        q_blk = pl.BlockSpec((1, tile, w), lambda bi, qi: (bi, qi, 0))
        qs, *softmax_state = state(tile)
        return pl.pallas_call(
            functools.partial(_attn_prompt_kernel, tile=tile, hda=hda, lam_init=lam_init),
            grid=(b, s // tile),
            in_specs=[q_blk, whole, whole] + small,
            out_specs=q_blk,
            out_shape=jax.ShapeDtypeStruct((b, s, w), BF16),
            scratch_shapes=[qs, pltpu.VMEM((s, w), BF16), pltpu.VMEM((s, w), BF16),
                            pltpu.VMEM((s // tile * (w // LANE), LANE, tile), BF16)] + softmax_state,
            compiler_params=_cparams("parallel", "arbitrary"),
            name="diff_attn",
        )(qa, ka, va, lam_vec, subln)
    ck, cv = cache
    tkc = min(CACHE_TILE, ck.shape[1])
    cache_blk = pl.BlockSpec((1, tkc, w), lambda bi, pi: (bi, pi, 0))
    return pl.pallas_call(
        functools.partial(_attn_cached_kernel, hda=hda, lam_init=lam_init),
        grid=(b, ck.shape[1] // tkc),
        in_specs=[whole, cache_blk, cache_blk, whole, whole] + small,
        out_specs=whole,
        out_shape=jax.ShapeDtypeStruct((b, s, w), BF16),
        scratch_shapes=state(s),
        compiler_params=_cparams("parallel", "arbitrary"),
        name="diff_attn_cached",
    )(qa, ck, cv, ka, va, lam_vec, subln)


def _retention_kernel(*refs, has_state, chunk):
    if has_state:
        q_ref, k_ref, v_ref, g_ref, rg_ref, s0_ref, y_ref, so_ref, st_ref, decay_ref = refs
    else:
        q_ref, k_ref, v_ref, g_ref, rg_ref, y_ref, so_ref, st_ref, decay_ref = refs
    ci = pl.program_id(1)
    n_heads = q_ref.shape[2] // LANE
    log_gammas = [float(np.log(1.0 - 2.0 ** (-5.0 - h))) for h in range(n_heads)]

    @pl.when(ci == 0)
    def _init():
        st_ref[...] = s0_ref[0] if has_state else jnp.zeros(st_ref.shape, F32)
        row = lax.broadcasted_iota(I32, (chunk, chunk), 0)
        col = lax.broadcasted_iota(I32, (chunk, chunk), 1)
        dist = (row - col).astype(F32)
        for h, lg in enumerate(log_gammas):
            decay_ref[h] = jnp.where(dist >= 0, jnp.exp(lg * jnp.maximum(dist, 0.0)), 0.0)

    idx = lax.broadcasted_iota(I32, (chunk, 1), 0).astype(F32)
    heads = list(enumerate(log_gammas))
    cols = [slice(h * LANE, (h + 1) * LANE) for h in range(n_heads)]
    qk = [lax.dot_general(q_ref[0, :, c], k_ref[0, :, c], (((1,), (1,)), ((), ())), preferred_element_type=F32)
          for c in cols]
    cross = [jnp.dot(q_ref[0, :, c], st_ref[h].astype(BF16), preferred_element_type=F32)
             for h, c in enumerate(cols)]
    kv = []
    for (h, lg), c in zip(heads, cols):
        kz = (k_ref[0, :, c].astype(F32) * jnp.exp(lg * (chunk - 1.0 - idx))).astype(BF16)
        kv.append(lax.dot_general(kz, v_ref[0, :, c], (((0,), (0,)), ((), ())), preferred_element_type=F32))
    intra = [jnp.dot((qk[h] * decay_ref[h]).astype(BF16), v_ref[0, :, c], preferred_element_type=F32)
             for h, c in enumerate(cols)]
    for (h, lg), c in zip(heads, cols):
        out = intra[h] + cross[h] * jnp.exp(lg * (idx + 1.0))
        st_ref[h] = float(np.exp(lg * chunk)) * st_ref[h] + kv[h]
        gate = g_ref[0, :, c].astype(F32)
        y_ref[0, :, c] = (_rms(out, rg_ref[...]) * (gate * jax.nn.sigmoid(gate))).astype(y_ref.dtype)

    @pl.when(ci == pl.num_programs(1) - 1)
    def _emit_state():
        so_ref[0] = st_ref[...]


def _retention(qb, kb, vb, gb, ret_g, state0):
    b, s, w = qb.shape
    n_heads = w // LANE
    chunk = min(RET_CHUNK, s)
    has_state = state0 is not None
    blk = pl.BlockSpec((1, chunk, w), lambda bi, ci: (bi, ci, 0))
    st_spec = pl.BlockSpec((1, n_heads, LANE, LANE), lambda bi, ci: (bi, 0, 0, 0))
    in_specs = [blk, blk, blk, blk, pl.BlockSpec((1, LANE), lambda bi, ci: (0, 0))]
    args = [qb, kb, vb, gb, ret_g]
    if has_state:
        in_specs.append(st_spec)
        args.append(state0)
    return pl.pallas_call(
        functools.partial(_retention_kernel, has_state=has_state, chunk=chunk),
        grid=(b, s // chunk),
        in_specs=in_specs,
        out_specs=[blk, st_spec],
        out_shape=[jax.ShapeDtypeStruct((b, s, w), BF16), jax.ShapeDtypeStruct((b, n_heads, LANE, LANE), F32)],
        scratch_shapes=[pltpu.VMEM((n_heads, LANE, LANE), F32), pltpu.VMEM((n_heads, chunk, chunk), F32)],
        compiler_params=_cparams("parallel", "arbitrary"),
        name="retention",
    )(*args)


def _router_logits(h_bf, wr_ref, br_ref):
    return jnp.dot(h_bf, wr_ref[...], preferred_element_type=F32) + br_ref[...]


def _first_index_of_max(vals, lane):
    vmax = jnp.max(vals, axis=1, keepdims=True)
    first = jnp.min(jnp.where(vals == vmax, lane.astype(F32), float(ROUTER_LANES)), axis=1, keepdims=True)
    return vmax, first.astype(I32)


def _mix_kernel(x_ref, ya_ref, yb_ref, woa_ref, wob_ref, g_ref, wr_ref, br_ref,
                x1_ref, cls_ref, cnt_ref, buf_ref, sem_ref, *, n_steps, n_groups, n_experts):
    x1 = (x_ref[...] + jnp.dot(ya_ref[...], woa_ref[...], preferred_element_type=F32)
          + jnp.dot(yb_ref[...], wob_ref[...], preferred_element_type=F32))

    def fill(slot):
        buf_ref[slot] = x1
    _emit_rows(buf_ref, x1_ref, sem_ref, n_steps, fill)
    logits = _router_logits(_rms(x1, g_ref[...]).astype(BF16), wr_ref, br_ref)
    lane = lax.broadcasted_iota(I32, logits.shape, 1)
    _, g_sel = _first_index_of_max(jnp.where(lane < n_groups, logits, NEG), lane)
    e_lo = n_groups + g_sel * n_experts
    e_logits = jnp.where((lane >= e_lo) & (lane < e_lo + n_experts), logits, NEG)
    _, top1 = _first_index_of_max(e_logits, lane)
    _, top2 = _first_index_of_max(jnp.where(lane == top1, NEG, e_logits), lane)
    ea = jnp.minimum(top1, top2) - e_lo
    eb = jnp.maximum(top1, top2) - e_lo
    pair = ((ea * (2 * n_experts - 1 - ea)) >> 1) + (eb - ea - 1)
    n_pairs = n_experts * (n_experts - 1) // 2
    cls = g_sel * n_pairs + pair
    onehot = (lane == cls).astype(F32)

    @pl.when(pl.program_id(0) == 0)
    def _zero():
        cnt_ref[...] = jnp.zeros(cnt_ref.shape, F32)

    cnt_ref[...] += jnp.sum(onehot, axis=0, keepdims=True)
    cls_rows = jnp.broadcast_to(cls.astype(F32), logits.shape).T
    cls_ref[0] = cls_rows[0:1].astype(I32)


def _mix(x2d, ya, yb, wo_a, wo_b, gain, wr, br, *, n_groups, n_experts):
    t, d = x2d.shape[0], gain.shape[1]
    ts = min(TOKEN_TILE, t)
    row = lambda i: (i, 0)
    const = lambda i: (0, 0)
    x1_shape = _row_tiled(t, d)
    return pl.pallas_call(
        functools.partial(_mix_kernel, n_steps=t // ts, n_groups=n_groups, n_experts=n_experts),
        grid=(t // ts,),
        in_specs=[pl.BlockSpec((ts, d), row), pl.BlockSpec((ts, ya.shape[1]), row),
                  pl.BlockSpec((ts, yb.shape[1]), row), pl.BlockSpec(wo_a.shape, const),
                  pl.BlockSpec(wo_b.shape, const), pl.BlockSpec((1, d), const),
                  pl.BlockSpec(wr.shape, const), pl.BlockSpec(br.shape, const)],
        out_specs=[pl.BlockSpec(memory_space=pl.ANY), pl.BlockSpec((1, 1, ts), lambda i: (i, 0, 0)),
                   pl.BlockSpec((1, ROUTER_LANES), const)],
        out_shape=[jax.ShapeDtypeStruct(x1_shape, F32), jax.ShapeDtypeStruct((t // ts, 1, ts), I32),
                   jax.ShapeDtypeStruct((1, ROUTER_LANES), F32)],
        scratch_shapes=[pltpu.VMEM((2, ts, d), F32), pltpu.SemaphoreType.DMA((2,))],
        compiler_params=_cparams("arbitrary"),
        name="mix",
    )(x2d, ya, yb, wo_a, wo_b, gain, wr, br)


ISSUE_UNROLL = 8
DMA_LANES = 2


def _row_copy(src, src_row, dst, dst_row, sem):
    return pltpu.make_async_copy(src.at[src_row], dst.at[dst_row], sem)


def _rows_copy(src, dst, n, sem):
    return pltpu.make_async_copy(src.at[pl.ds(0, n)], dst.at[pl.ds(0, n)], sem)


def _dispatch_kernel(start_ref, end_ref, cls_ref, x_ref, xs_ref, pos_ref, next_ref, zero_ref, sem, *, n_cls):
    step = pl.program_id(0)
    ts = x_ref.shape[0]

    @pl.when(step == 0)
    def _init():
        zero_ref[...] = jnp.zeros(zero_ref.shape, F32)

        def seed(c, carry):
            next_ref[c] = start_ref[c]
            return carry
        lax.fori_loop(0, n_cls, seed, 0)

    def place(j, carry):
        for lane in range(DMA_LANES):
            r = j * DMA_LANES + lane
            c = cls_ref[0, 0, r]
            slot = next_ref[c]
            next_ref[c] = slot + 1
            pos_ref[0, 0, r] = slot
            _row_copy(x_ref, r, xs_ref, slot, sem).start(priority=lane)
        return carry
    lax.fori_loop(0, ts // DMA_LANES, place, 0, unroll=ISSUE_UNROLL // DMA_LANES)
    _rows_copy(x_ref, xs_ref, ts, sem).wait()

    @pl.when(step == pl.num_programs(0) - 1)
    def _pad():
        def per_class(c, carry):
            lo, hi = next_ref[c], end_ref[c]

            def fill(slot, inner):
                _row_copy(zero_ref, 0, xs_ref, slot, sem).start()
                return inner
            lax.fori_loop(lo, hi, fill, 0)

            def fill_wait(slot, inner):
                _row_copy(zero_ref, 0, xs_ref, 0, sem).wait()
                return inner
            lax.fori_loop(lo, hi, fill_wait, 0)
            return carry
        lax.fori_loop(0, n_cls, per_class, 0)

        tile = zero_ref.shape[0]
        first_free = lax.div(end_ref[n_cls - 1], tile)
        n_tiles = xs_ref.shape[0] // tile

        def tile_copy(ti):
            return pltpu.make_async_copy(zero_ref, xs_ref.at[pl.ds(pl.multiple_of(ti * tile, tile), tile)], sem)

        def clear(ti, carry):
            tile_copy(ti).start()
            return carry
        lax.fori_loop(first_free, n_tiles, clear, 0)

        def clear_wait(ti, carry):
            tile_copy(ti).wait()
            return carry
        lax.fori_loop(first_free, n_tiles, clear_wait, 0)


def _dispatch(x1, cls, slot_start, slot_end, n_slots, *, n_cls, tile):
    n_tiles, _, ts = cls.shape
    smem_blk = pl.BlockSpec((1, 1, ts), lambda i, *_: (i, 0, 0), memory_space=pltpu.SMEM)
    return pl.pallas_call(
        functools.partial(_dispatch_kernel, n_cls=n_cls),
        grid_spec=pltpu.PrefetchScalarGridSpec(
            num_scalar_prefetch=2,
            grid=(n_tiles,),
            in_specs=[smem_blk, _rows_spec(ts, x1.shape)],
            out_specs=[pl.BlockSpec(memory_space=pl.ANY), smem_blk],
            scratch_shapes=[pltpu.SMEM((ROUTER_LANES,), I32), pltpu.VMEM((tile, *x1.shape[1:]), F32),
                            pltpu.SemaphoreType.DMA],
        ),
        out_shape=[jax.ShapeDtypeStruct((n_slots, *x1.shape[1:]), F32), jax.ShapeDtypeStruct(cls.shape, I32)],
        compiler_params=_cparams("arbitrary"),
        name="dispatch",
    )(slot_start, slot_end, cls, x1)


def _gather_rows(pos_ref, ys_ref, rows_ref, sem):
    ts = rows_ref.shape[0]

    def fetch(j, carry):
        for lane in range(DMA_LANES):
            r = j * DMA_LANES + lane
            _row_copy(ys_ref, pos_ref[0, 0, r], rows_ref, r, sem).start(priority=lane)
        return carry
    lax.fori_loop(0, ts // DMA_LANES, fetch, 0, unroll=ISSUE_UNROLL // DMA_LANES)
    _rows_copy(ys_ref, rows_ref, ts, sem).wait()


def _unpermute_kernel(pos_ref, ys_ref, o_ref, stage_ref, sem, out_sem, *, n_steps):
    def fill(slot):
        _gather_rows(pos_ref, ys_ref, stage_ref.at[slot], sem)
    _emit_rows(stage_ref, o_ref, out_sem, n_steps, fill)


def _unpermute(ys, pos, t):
    n_tiles, _, ts = pos.shape
    hbm = pl.BlockSpec(memory_space=pl.ANY)
    return pl.pallas_call(
        functools.partial(_unpermute_kernel, n_steps=n_tiles),
        grid=(n_tiles,),
        in_specs=[pl.BlockSpec((1, 1, ts), lambda i: (i, 0, 0), memory_space=pltpu.SMEM), hbm],
        out_specs=hbm,
        out_shape=jax.ShapeDtypeStruct((t, ys.shape[1] * ys.shape[2]), F32),
        scratch_shapes=[pltpu.VMEM((2, ts, *ys.shape[1:]), F32), pltpu.SemaphoreType.DMA,
                        pltpu.SemaphoreType.DMA((2,))],
        compiler_params=_cparams("arbitrary"),
        name="unpermute",
    )(pos, ys)


def _expert_kernel(tg_ref, ta_ref, tb_ref, nv_ref, xs_ref, g_ref, wr_ref, br_ref,
                   wga_ref, wua_ref, wda_ref, wgb_ref, wub_ref, wdb_ref, ys_ref,
                   xbuf_ref, in_sem_ref, buf_ref, sem_ref, *, n_steps, n_groups, n_experts):
    i = pl.program_id(0)
    in_slot = _fetch_rows(xs_ref, xbuf_ref, in_sem_ref, n_steps)

    def run():
        x = xbuf_ref[in_slot]
        h = _rms(x, g_ref[...]).astype(BF16)
        logits = _router_logits(h, wr_ref, br_ref)
        gates = [jnp.dot(h, w[0, 0], preferred_element_type=F32) for w in (wga_ref, wgb_ref)]
        ups = [jnp.dot(h, w[0, 0], preferred_element_type=F32) for w in (wua_ref, wub_ref)]
        lane = lax.broadcasted_iota(I32, logits.shape, 1)
        grp, ea, eb = tg_ref[i], ta_ref[i], tb_ref[i]

        def pick(idx):
            return jnp.sum(jnp.where(lane == idx, logits, 0.0), axis=1, keepdims=True)

        g_logits = jnp.where(lane < n_groups, logits, NEG)
        g_max = jnp.max(g_logits, axis=1, keepdims=True)
        g_w = jnp.exp(pick(grp) - g_max) / jnp.sum(jnp.exp(g_logits - g_max), axis=1, keepdims=True)
        e_lo = n_groups + grp * n_experts
        w_a = jax.nn.sigmoid(pick(e_lo + ea) - pick(e_lo + eb))
        acts = [(gate * jax.nn.sigmoid(gate) * up * comb).astype(BF16)
                for gate, up, comb in zip(gates, ups, (g_w * w_a, g_w * (1.0 - w_a)))]
        return (x + jnp.dot(acts[0], wda_ref[0, 0], preferred_element_type=F32)
                + jnp.dot(acts[1], wdb_ref[0, 0], preferred_element_type=F32))

    def fill(slot):
        @pl.when(nv_ref[i] == 0)
        def _unused_tile():
            buf_ref[slot] = jnp.zeros(buf_ref.shape[1:], F32)

        @pl.when(nv_ref[i] > 0)
        def _used_tile():
            buf_ref[slot] = run()
    _emit_rows(buf_ref, ys_ref, sem_ref, n_steps, fill)


def _experts(xs, gain, wr, br, wg, wu, wd, meta, *, tile, n_groups, n_experts):
    n_slots, d = xs.shape[0], gain.shape[1]
    de = wg.shape[-1]
    n_steps = n_slots // tile
    tile_g, tile_a, tile_b, tile_nv = meta
    const = lambda i, *_: (0, 0)
    sel_a = lambda i, tg, ta, tb, nv: (tg[i], ta[i], 0, 0)
    sel_b = lambda i, tg, ta, tb, nv: (tg[i], tb[i], 0, 0)
    up_spec_a = pl.BlockSpec((1, 1, d, de), sel_a)
    up_spec_b = pl.BlockSpec((1, 1, d, de), sel_b)
    hbm = pl.BlockSpec(memory_space=pl.ANY)
    return pl.pallas_call(
        functools.partial(_expert_kernel, n_steps=n_steps, n_groups=n_groups, n_experts=n_experts),
        grid_spec=pltpu.PrefetchScalarGridSpec(
            num_scalar_prefetch=4,
            grid=(n_steps,),
            in_specs=[hbm, pl.BlockSpec((1, d), const),
                      pl.BlockSpec(wr.shape, const), pl.BlockSpec(br.shape, const),
                      up_spec_a, up_spec_a, pl.BlockSpec((1, 1, de, d), sel_a),
                      up_spec_b, up_spec_b, pl.BlockSpec((1, 1, de, d), sel_b)],
            out_specs=hbm,
            scratch_shapes=[pltpu.VMEM((2, tile, d), F32), pltpu.SemaphoreType.DMA((2,)),
                            pltpu.VMEM((2, tile, d), F32), pltpu.SemaphoreType.DMA((2,))],
        ),
        out_shape=jax.ShapeDtypeStruct(xs.shape, F32),
        compiler_params=_cparams("arbitrary"),
        name="experts",
    )(tile_g, tile_a, tile_b, tile_nv, xs, gain, wr, br, wg, wu, wd, wg, wu, wd)


def _class_layout(counts, n_cls, n_pairs, n_experts, n_tiles, tile):
    counts = counts[0, :n_cls].astype(I32)
    tiles_c = (counts + tile - 1) // tile
    tile_end = jnp.cumsum(tiles_c)
    tile_start = tile_end - tiles_c
    n_active = tile_end[-1]
    pad = ROUTER_LANES - n_cls
    slot_start = jnp.pad(tile_start * tile, (0, pad))
    slot_end = jnp.pad(tile_end * tile, (0, pad))
    t_idx = jnp.arange(n_tiles, dtype=I32)
    t_cls = jnp.minimum(jnp.sum((tile_end[None, :] <= t_idx[:, None]).astype(I32), axis=1), n_cls - 1)
    active = t_idx < n_active
    t_nv = jnp.where(active, jnp.clip(counts[t_cls] - (t_idx - tile_start[t_cls]) * tile, 0, tile), 0)
    t_cls = jnp.where(active, t_cls, t_cls[jnp.maximum(n_active - 1, 0)])
    pair_a, pair_b = np.triu_indices(n_experts, 1)
    pair = t_cls % n_pairs
    meta = (t_cls // n_pairs, jnp.asarray(pair_a, I32)[pair], jnp.asarray(pair_b, I32)[pair], t_nv.astype(I32))
    return slot_start.astype(I32), slot_end.astype(I32), meta


def _rope_tables(pos, group, signed_half):
    half = group // 2
    lane = np.arange(LANE)
    inv = (ROPE_THETA ** (-jnp.arange(0, group, 2, dtype=F32) / group))[lane % half]
    ang = pos.astype(F32)[:, None] * inv[None, :]
    sign = np.where((lane % group) < signed_half, -1.0, 1.0).astype(np.float32)
    return jnp.cos(ang), jnp.sin(ang) * sign[None, :]


def _layer(x2d, b, s, pos, cache, state0, p, li, dims):
    aw, bw, hda, dkb = dims["aw"], dims["bw"], dims["hda"], dims["dkb"]
    n_groups, n_experts = dims["n_groups"], dims["n_experts"]
    lam_init = 0.8 - 0.6 * float(np.exp(-0.3 * li))
    t = b * s
    tabs = (*_rope_tables(pos, hda, hda // 2), *_rope_tables(pos, dkb, dkb // 2))
    qa, ka, va, qb, kb, vb, gb = _project(x2d, s, p["norm_attn"], p["w_in"], p["gm"], p["gq"], p["gk"], tabs,
                                          aw=aw, bw=bw, hda=hda, dkb=dkb)
    shape3 = lambda a: a.reshape(b, s, a.shape[-1])
    ya = _diff_attention(shape3(qa), shape3(ka), shape3(va), cache, p["lam_vec"], p["subln"],
                         causal=cache is None, hda=hda, lam_init=lam_init)
    yb, new_state = _retention(shape3(qb), shape3(kb), shape3(vb), shape3(gb), p["ret_norm"], state0)
    x1, cls, counts = _mix(x2d, ya.reshape(t, aw), yb.reshape(t, bw), p["wo_a"], p["wo_b"], p["norm_ffn"],
                           p["wr"], p["br"], n_groups=n_groups, n_experts=n_experts)
    n_pairs = n_experts * (n_experts - 1) // 2
    n_cls = n_groups * n_pairs
    tile = EXPERT_TILE if t >= EXPERT_TILE * n_cls else SMALL_EXPERT_TILE
    n_tiles = t // tile + n_cls
    slot_start, slot_end, meta = _class_layout(counts, n_cls, n_pairs, n_experts, n_tiles, tile)
    xs, slot_of = _dispatch(x1, cls, slot_start, slot_end, n_tiles * tile, n_cls=n_cls, tile=tile)
    ys = _experts(xs, p["norm_ffn"], p["wr"], p["br"], p["w_gate"], p["w_up"], p["w_down"], meta,
                  tile=tile, n_groups=n_groups, n_experts=n_experts)
    return _unpermute(ys, slot_of, t), ka, va, new_state


def kernel(x_prompt, x_sample, cache_k, cache_v, state_ret, norm_attn, w_in, q_norm, k_norm, lam_vec,
           subln, ret_norm, w_out, norm_ffn, w_group, b_group, w_expert, b_expert, w_gate, w_up, w_down):
    depth, dec_b, past, heads2, hda = cache_k.shape
    _, _, n_ret, dkb, dvb = state_ret.shape
    n_groups, n_experts = w_gate.shape[1], w_gate.shape[2]
    aw, bw = heads2 * hda, n_ret * dkb
    assert 2 * hda == LANE and dkb == LANE and dvb == LANE and aw % LANE == 0
    assert n_groups * (1 + n_experts) <= ROUTER_LANES
    assert n_groups * n_experts * (n_experts - 1) // 2 <= ROUTER_LANES
    dims = dict(aw=aw, bw=bw, hda=hda, dkb=dkb, n_groups=n_groups, n_experts=n_experts)
    d = x_prompt.shape[-1]
    group_of = np.arange(aw) // hda
    gm = jnp.asarray((group_of[:, None] == group_of[None, :]) / hda, BF16)
    row = lambda v: v.reshape(1, -1).astype(F32)

    def layer_params(li):
        wr = jnp.concatenate([w_group[li], w_expert[li].reshape(d, n_groups * n_experts)], axis=1)
        br = jnp.concatenate([b_group[li], b_expert[li].reshape(-1)])
        lane_pad = ROUTER_LANES - wr.shape[1]
        return dict(
            norm_attn=row(norm_attn[li]), w_in=w_in[li].astype(BF16), gm=gm,
            gq=row(jnp.tile(q_norm[li], heads2)), gk=row(jnp.tile(k_norm[li], heads2)),
            lam_vec=lam_vec[li].astype(F32), subln=row(subln[li]), ret_norm=row(ret_norm[li]),
            wo_a=w_out[li, :aw].astype(BF16), wo_b=w_out[li, aw:].astype(BF16), norm_ffn=row(norm_ffn[li]),
            wr=jnp.pad(wr, ((0, 0), (0, lane_pad))).astype(BF16), br=row(jnp.pad(br, (0, lane_pad))),
            w_gate=w_gate[li].astype(BF16), w_up=w_up[li].astype(BF16), w_down=w_down[li].astype(BF16))

    pos_p = jnp.arange(x_prompt.shape[1])
    pos_s = past + jnp.arange(x_sample.shape[1])
    (bp, sp), (bs, ss) = x_prompt.shape[:2], x_sample.shape[:2]
    xp, xs = x_prompt.reshape(bp * sp, d), x_sample.reshape(bs * ss, d)
    outs = [[] for _ in range(6)]
    for li in range(depth):
        p = layer_params(li)
        xp, ka, va, st = _layer(xp, bp, sp, pos_p, None, None, p, li, dims)
        outs[0].append(ka.reshape(bp, sp, heads2, hda))
        outs[1].append(va.reshape(bp, sp, heads2 // 2, 2 * hda))
        outs[2].append(st)
        cache = (cache_k[li].reshape(dec_b, past, aw), cache_v[li].reshape(dec_b, past, aw))
        xs, ka, va, st = _layer(xs, bs, ss, pos_s, cache, state_ret[li], p, li, dims)
        outs[3].append(ka.reshape(bs, ss, heads2, hda))
        outs[4].append(va.reshape(bs, ss, heads2 // 2, 2 * hda))
        outs[5].append(st)
    return (xp.reshape(bp, sp, d), xs.reshape(bs, ss, d), *(jnp.stack(o) for o in outs))
```

```python
import functools

import jax
import jax.numpy as jnp
import numpy as np
from jax import lax
from jax.experimental import pallas as pl
from jax.experimental.pallas import tpu as pltpu

F32 = jnp.float32
BF16 = jnp.bfloat16
I32 = jnp.int32

CHUNK = 64
ROPE_THETA = 10000.0
EPS = 1e-6
NEG = -1e30
LANE = 128
SUBLANES = 8
ROUTER_LANES = 128
VMEM_LIMIT_BYTES = 52 * 1024 * 1024

TOKEN_TILE = 512
ATTN_TILE = 256
CACHE_TILE = 512
LOG2E = 1.4426950408889634
RET_CHUNK = 256
EXPERT_TILE = 256
SMALL_EXPERT_TILE = 16


def _cparams(*sem):
    return pltpu.CompilerParams(dimension_semantics=sem, vmem_limit_bytes=VMEM_LIMIT_BYTES)


def _rms(x, gain):
    ms = jnp.mean(x * x, axis=-1, keepdims=True)
    return x * lax.rsqrt(ms + EPS) * gain


def _rot_half(y, half):
    if 2 * half == LANE:
        return pltpu.roll(y, half, axis=1)
    lane = lax.broadcasted_iota(I32, y.shape, 1)
    first = (lane & (2 * half - 1)) < half
    return jnp.where(first, pltpu.roll(y, LANE - half, axis=1), pltpu.roll(y, half, axis=1))


def _row_tiled(t, d):
    assert d % (SUBLANES * LANE) == 0
    return (t, d // LANE, LANE)


def _rows_spec(ts, shape):
    zeros = (0,) * (len(shape) - 1)
    return pl.BlockSpec((ts, *shape[1:]), lambda i, *s: (i, *zeros))


def _retile_copies(buf_ref, slot, out_ref, row0, sem_ref):
    rows = buf_ref.shape[1]
    copies = []
    for c in range(out_ref.shape[1] if len(out_ref.shape) == 3 else buf_ref.shape[2]):
        lanes = pl.ds(c * LANE, LANE)
        if len(out_ref.shape) == 3:
            src, dst = buf_ref.at[slot, :, lanes], out_ref.at[pl.ds(row0, rows), c, :]
        else:
            src, dst = buf_ref.at[slot, :, c, :], out_ref.at[pl.ds(row0, rows), lanes]
        copies.append(pltpu.make_async_copy(src, dst, sem_ref.at[slot]))
    return copies


def _fetch_rows(src_ref, buf_ref, sem_ref, n_steps):
    step = pl.program_id(0)
    slot = lax.rem(step, 2)
    rows = buf_ref.shape[1]

    def copies(s, row0):
        return [pltpu.make_async_copy(src_ref.at[pl.ds(row0, rows), c, :],
                                      buf_ref.at[s, :, pl.ds(c * LANE, LANE)], sem_ref.at[s])
                for c in range(src_ref.shape[1])]

    @pl.when(step == 0)
    def _first():
        for cp in copies(0, 0):
            cp.start()

    @pl.when(step + 1 < n_steps)
    def _next():
        for cp in copies(1 - slot, pl.multiple_of((step + 1) * rows, rows)):
            cp.start()

    for cp in copies(slot, 0):
        cp.wait()
    return slot


def _emit_rows(buf_ref, out_ref, sem_ref, n_steps, fill):
    step = pl.program_id(0)
    slot = lax.rem(step, 2)
    rows = buf_ref.shape[1]

    def drain(s):
        for cp in _retile_copies(buf_ref, s, out_ref, 0, sem_ref):
            cp.wait()

    @pl.when(step >= 2)
    def _reuse():
        drain(slot)

    fill(slot)
    for cp in _retile_copies(buf_ref, slot, out_ref, pl.multiple_of(step * rows, rows), sem_ref):
        cp.start()

    @pl.when(step == n_steps - 1)
    def _finish():
        drain(slot)
        if n_steps >= 2:
            drain(1 - slot)


def _proj_kernel(x_ref, g_ref, w_ref, gm_ref, gq_ref, gk_ref, ca_ref, sa_ref, cb_ref, sb_ref,
                 qa_ref, ka_ref, va_ref, qb_ref, kb_ref, vb_ref, gb_ref, *, aw, bw, hda, dkb):
    h = _rms(x_ref[...], g_ref[...]).astype(BF16)

    def proj(c0, width):
        return jnp.dot(h, w_ref[:, c0:c0 + width], preferred_element_type=F32)

    def qk_norm_rot(z, gain_ref, out_ref):
        ss = jnp.dot((z * z).astype(BF16), gm_ref[...], preferred_element_type=F32)
        y = z * lax.rsqrt(ss + EPS) * gain_ref[...]
        for c in range(aw // LANE):
            yc = y[:, c * LANE:(c + 1) * LANE]
            r = yc * ca_ref[...] + _rot_half(yc, hda // 2) * sa_ref[...]
            out_ref[:, c * LANE:(c + 1) * LANE] = r.astype(out_ref.dtype)

    def rot_b(z, out_ref, scale):
        for c in range(bw // LANE):
            zc = z[:, c * LANE:(c + 1) * LANE]
            r = zc * cb_ref[...] + _rot_half(zc, dkb // 2) * sb_ref[...]
            out_ref[:, c * LANE:(c + 1) * LANE] = (r * scale).astype(out_ref.dtype)

    qk_norm_rot(proj(0, aw), gq_ref, qa_ref)
    qk_norm_rot(proj(aw, aw), gk_ref, ka_ref)
    va_ref[...] = proj(2 * aw, aw)
    rot_b(proj(3 * aw, bw), qb_ref, 1.0)
    rot_b(proj(3 * aw + bw, bw), kb_ref, dkb ** -0.5)
    vb_ref[...] = proj(3 * aw + 2 * bw, bw).astype(BF16)
    gb_ref[...] = proj(3 * aw + 3 * bw, bw).astype(BF16)


def _project(x2d, seq, gain, w_bf, gm, gq_t, gk_t, tabs, *, aw, bw, hda, dkb):
    t, d = x2d.shape[0], gain.shape[1]
    ts = min(TOKEN_TILE, seq)
    n_s = seq // ts
    row = lambda i: (i, 0)
    const = lambda i: (0, 0)
    tab = lambda i: (i % n_s, 0)
    tab_spec = pl.BlockSpec((ts, LANE), tab)
    out_a = pl.BlockSpec((ts, aw), row)
    out_b = pl.BlockSpec((ts, bw), row)
    return pl.pallas_call(
        functools.partial(_proj_kernel, aw=aw, bw=bw, hda=hda, dkb=dkb),
        grid=(t // ts,),
        in_specs=[pl.BlockSpec((ts, d), row), pl.BlockSpec((1, d), const),
                  pl.BlockSpec(w_bf.shape, const), pl.BlockSpec(gm.shape, const),
                  pl.BlockSpec((1, aw), const), pl.BlockSpec((1, aw), const),
                  tab_spec, tab_spec, tab_spec, tab_spec],
        out_specs=[out_a, out_a, out_a, out_b, out_b, out_b, out_b],
        out_shape=[jax.ShapeDtypeStruct((t, aw), BF16), jax.ShapeDtypeStruct((t, aw), F32),
                   jax.ShapeDtypeStruct((t, aw), F32), jax.ShapeDtypeStruct((t, bw), BF16),
                   jax.ShapeDtypeStruct((t, bw), BF16), jax.ShapeDtypeStruct((t, bw), BF16),
                   jax.ShapeDtypeStruct((t, bw), BF16)],
        compiler_params=_cparams("parallel"),
        name="proj",
    )(x2d, gain, w_bf, gm, gq_t, gk_t, *tabs)


def _scaled_queries(q, hda):
    return (q.astype(F32) * (hda ** -0.5 * LOG2E)).astype(BF16)


def _split_halves(k, hda):
    lo = (lax.broadcasted_iota(I32, k.shape, 1) & (LANE - 1)) < hda
    return jnp.where(lo, k, 0.0).astype(BF16), jnp.where(lo, 0.0, k).astype(BF16)


def _reset_softmax_state(m_ref, l_ref, acc_ref):
    m_ref[...] = jnp.full(m_ref.shape, NEG, F32)
    l_ref[...] = jnp.zeros(l_ref.shape, F32)
    acc_ref[...] = jnp.zeros(acc_ref.shape, F32)


def _softmax_blocks(qs, blocks, m_ref, l_ref, acc_ref):
    scores = [[lax.dot_general(kk, q, (((1,), (1,)), ((), ())), preferred_element_type=F32)
               for q, halves in zip(qs, k_halves) for kk in halves] for k_halves, _, _ in blocks]
    for (_, vts, valid), block_scores in zip(blocks, scores):
        probs, alphas = [], []
        for c, s in enumerate(block_scores):
            if valid is not None:
                s = jnp.where(valid, s, NEG)
            m_old = m_ref[c]
            m_new = jnp.maximum(m_old, jnp.max(s, axis=0, keepdims=True))
            alpha = jnp.exp2(m_old - m_new)
            p = jnp.exp2(s - m_new)
            l_ref[c] = alpha * l_ref[c] + jnp.sum(p, axis=0, keepdims=True)
            m_ref[c] = m_new
            probs.append(p.astype(BF16))
            alphas.append(alpha)
        for c, (p, alpha) in enumerate(zip(probs, alphas)):
            acc_ref[c] = alpha * acc_ref[c] + jnp.dot(vts[c // 2], p, preferred_element_type=F32)


def _finish_heads(lv_ref, sg_ref, o_ref, l_ref, acc_ref, lam_init):
    lv = lv_ref[...]
    lam = (jnp.exp(jnp.sum(lv[0:1] * lv[1:2], axis=1, keepdims=True))
           - jnp.exp(jnp.sum(lv[2:3] * lv[3:4], axis=1, keepdims=True)) + lam_init)
    for h in range(o_ref.shape[2] // LANE):
        o_t = acc_ref[2 * h] / l_ref[2 * h] - lam * (acc_ref[2 * h + 1] / l_ref[2 * h + 1])
        o_ref[0, :, h * LANE:(h + 1) * LANE] = (_rms(o_t.T, sg_ref[...]) * (1.0 - lam_init)).astype(o_ref.dtype)


def _attn_prompt_kernel(q_ref, k_ref, v_ref, lv_ref, sg_ref, o_ref,
                        qs_ref, klo_ref, khi_ref, vt_ref, m_ref, l_ref, acc_ref, *, tile, hda, lam_init):
    qi = pl.program_id(1)
    n_pairs = q_ref.shape[2] // LANE
    n_blk = k_ref.shape[1] // tile

    @pl.when(qi == 0)
    def _stage_keys():
        def stage(j, carry):
            rows = pl.ds(pl.multiple_of(j * tile, tile), tile)
            klo_ref[rows, :], khi_ref[rows, :] = _split_halves(k_ref[0, rows, :], hda)
            v = v_ref[0, rows, :]
            for h in range(n_pairs):
                vt_ref[j * n_pairs + h] = v[:, h * LANE:(h + 1) * LANE].T.astype(BF16)
            return carry
        lax.fori_loop(0, n_blk, stage, 0)

    qs_ref[...] = _scaled_queries(q_ref[0], hda)
    _reset_softmax_state(m_ref, l_ref, acc_ref)

    cols = [slice(h * LANE, (h + 1) * LANE) for h in range(n_pairs)]

    def key_block(kb, valid):
        rows = pl.ds(pl.multiple_of(kb * tile, tile), tile)
        return ([(klo_ref[rows, c], khi_ref[rows, c]) for c in cols],
                [vt_ref[kb * n_pairs + h] for h in range(n_pairs)], valid)

    def attend(blocks):
        _softmax_blocks([qs_ref[:, c] for c in cols], blocks, m_ref, l_ref, acc_ref)

    def past_pair(j, carry):
        attend([key_block(2 * j, None), key_block(2 * j + 1, None)])
        return carry
    lax.fori_loop(0, qi // 2, past_pair, 0)
    shift = CHUNK.bit_length() - 1
    kchunk = lax.broadcasted_iota(I32, (tile, tile), 0) >> shift
    qchunk = lax.broadcasted_iota(I32, (tile, tile), 1) >> shift
    own = kchunk <= qchunk

    @pl.when(qi % 2 == 1)
    def _odd():
        attend([key_block(qi - 1, None), key_block(qi, own)])

    @pl.when(qi % 2 == 0)
    def _even():
        attend([key_block(qi, own)])
    _finish_heads(lv_ref, sg_ref, o_ref, l_ref, acc_ref, lam_init)


def _attn_cached_kernel(q_ref, kc_ref, vc_ref, kn_ref, vn_ref, lv_ref, sg_ref, o_ref,
                        qs_ref, m_ref, l_ref, acc_ref, *, hda, lam_init):
    pi = pl.program_id(1)
    n_pairs = q_ref.shape[2] // LANE

    @pl.when(pi == 0)
    def _start():
        qs_ref[...] = _scaled_queries(q_ref[0], hda)
        _reset_softmax_state(m_ref, l_ref, acc_ref)

    def key_block(k_src, v_src):
        k_lo, k_hi = _split_halves(k_src[0], hda)
        v = v_src[0]
        cols = [slice(h * LANE, (h + 1) * LANE) for h in range(n_pairs)]
        block = ([(k_lo[:, c], k_hi[:, c]) for c in cols], [v[:, c].T.astype(BF16) for c in cols], None)
        _softmax_blocks([qs_ref[:, c] for c in cols], [block], m_ref, l_ref, acc_ref)

    key_block(kc_ref, vc_ref)

    @pl.when(pi == pl.num_programs(1) - 1)
    def _finish():
        key_block(kn_ref, vn_ref)
        _finish_heads(lv_ref, sg_ref, o_ref, l_ref, acc_ref, lam_init)


def _diff_attention(qa, ka, va, cache, lam_vec, subln, *, causal, hda, lam_init):
    b, s, w = qa.shape
    n_chains = 2 * (w // LANE)
    const = lambda bi, si: (0, 0)
    whole = pl.BlockSpec((1, s, w), lambda bi, si: (bi, 0, 0))
    small = [pl.BlockSpec(lam_vec.shape, const), pl.BlockSpec((1, LANE), const)]

    def state(tq):
        return [pltpu.VMEM((tq, w), BF16), pltpu.VMEM((n_chains, 1, tq), F32),
                pltpu.VMEM((n_chains, 1, tq), F32), pltpu.VMEM((n_chains, LANE, tq), F32)]

    if causal:
        tile = min(ATTN_TILE, s)
        q_blk = pl.BlockSpec((1, tile, w), lambda bi, qi: (bi, qi, 0))
        qs, *softmax_state = state(tile)
        return pl.pallas_call(
            functools.partial(_attn_prompt_kernel, tile=tile, hda=hda, lam_init=lam_init),
            grid=(b, s // tile),
            in_specs=[q_blk, whole, whole] + small,
            out_specs=q_blk,
            out_shape=jax.ShapeDtypeStruct((b, s, w), BF16),
            scratch_shapes=[qs, pltpu.VMEM((s, w), BF16), pltpu.VMEM((s, w), BF16),
                            pltpu.VMEM((s // tile * (w // LANE), LANE, tile), BF16)] + softmax_state,
            compiler_params=_cparams("parallel", "arbitrary"),
            name="diff_attn",
        )(qa, ka, va, lam_vec, subln)
    ck, cv = cache
    tkc = min(CACHE_TILE, ck.shape[1])
    cache_blk = pl.BlockSpec((1, tkc, w), lambda bi, pi: (bi, pi, 0))
    return pl.pallas_call(
        functools.partial(_attn_cached_kernel, hda=hda, lam_init=lam_init),
        grid=(b, ck.shape[1] // tkc),
        in_specs=[whole, cache_blk, cache_blk, whole, whole] + small,
        out_specs=whole,
        out_shape=jax.ShapeDtypeStruct((b, s, w), BF16),
        scratch_shapes=state(s),
        compiler_params=_cparams("parallel", "arbitrary"),
        name="diff_attn_cached",
    )(qa, ck, cv, ka, va, lam_vec, subln)


def _retention_kernel(*refs, has_state, chunk):
    if has_state:
        q_ref, k_ref, v_ref, g_ref, rg_ref, s0_ref, y_ref, so_ref, st_ref, decay_ref = refs
    else:
        q_ref, k_ref, v_ref, g_ref, rg_ref, y_ref, so_ref, st_ref, decay_ref = refs
    ci = pl.program_id(1)
    n_heads = q_ref.shape[2] // LANE
    log_gammas = [float(np.log(1.0 - 2.0 ** (-5.0 - h))) for h in range(n_heads)]

    @pl.when(ci == 0)
    def _init():
        st_ref[...] = s0_ref[0] if has_state else jnp.zeros(st_ref.shape, F32)
        row = lax.broadcasted_iota(I32, (chunk, chunk), 0)
        col = lax.broadcasted_iota(I32, (chunk, chunk), 1)
        dist = (row - col).astype(F32)
        for h, lg in enumerate(log_gammas):
            decay_ref[h] = jnp.where(dist >= 0, jnp.exp(lg * jnp.maximum(dist, 0.0)), 0.0)

    idx = lax.broadcasted_iota(I32, (chunk, 1), 0).astype(F32)
    heads = list(enumerate(log_gammas))
    cols = [slice(h * LANE, (h + 1) * LANE) for h in range(n_heads)]
    qk = [lax.dot_general(q_ref[0, :, c], k_ref[0, :, c], (((1,), (1,)), ((), ())), preferred_element_type=F32)
          for c in cols]
    cross = [jnp.dot(q_ref[0, :, c], st_ref[h].astype(BF16), preferred_element_type=F32)
             for h, c in enumerate(cols)]
    kv = []
    for (h, lg), c in zip(heads, cols):
        kz = (k_ref[0, :, c].astype(F32) * jnp.exp(lg * (chunk - 1.0 - idx))).astype(BF16)
        kv.append(lax.dot_general(kz, v_ref[0, :, c], (((0,), (0,)), ((), ())), preferred_element_type=F32))
    intra = [jnp.dot((qk[h] * decay_ref[h]).astype(BF16), v_ref[0, :, c], preferred_element_type=F32)
             for h, c in enumerate(cols)]
    for (h, lg), c in zip(heads, cols):
        out = intra[h] + cross[h] * jnp.exp(lg * (idx + 1.0))
        st_ref[h] = float(np.exp(lg * chunk)) * st_ref[h] + kv[h]
        gate = g_ref[0, :, c].astype(F32)
        y_ref[0, :, c] = (_rms(out, rg_ref[...]) * (gate * jax.nn.sigmoid(gate))).astype(y_ref.dtype)

    @pl.when(ci == pl.num_programs(1) - 1)
    def _emit_state():
        so_ref[0] = st_ref[...]


def _retention(qb, kb, vb, gb, ret_g, state0):
    b, s, w = qb.shape
    n_heads = w // LANE
    chunk = min(RET_CHUNK, s)
    has_state = state0 is not None
    blk = pl.BlockSpec((1, chunk, w), lambda bi, ci: (bi, ci, 0))
    st_spec = pl.BlockSpec((1, n_heads, LANE, LANE), lambda bi, ci: (bi, 0, 0, 0))
    in_specs = [blk, blk, blk, blk, pl.BlockSpec((1, LANE), lambda bi, ci: (0, 0))]
    args = [qb, kb, vb, gb, ret_g]
    if has_state:
        in_specs.append(st_spec)
        args.append(state0)
    return pl.pallas_call(
        functools.partial(_retention_kernel, has_state=has_state, chunk=chunk),
        grid=(b, s // chunk),
        in_specs=in_specs,
        out_specs=[blk, st_spec],
        out_shape=[jax.ShapeDtypeStruct((b, s, w), BF16), jax.ShapeDtypeStruct((b, n_heads, LANE, LANE), F32)],
        scratch_shapes=[pltpu.VMEM((n_heads, LANE, LANE), F32), pltpu.VMEM((n_heads, chunk, chunk), F32)],
        compiler_params=_cparams("parallel", "arbitrary"),
        name="retention",
    )(*args)


def _router_logits(h_bf, wr_ref, br_ref):
    return jnp.dot(h_bf, wr_ref[...], preferred_element_type=F32) + br_ref[...]


def _first_row_of_max(vals, row):
    vmax = jnp.max(vals, axis=0, keepdims=True)
    first = jnp.min(jnp.where(vals == vmax, row.astype(F32), float(ROUTER_LANES)), axis=0, keepdims=True)
    return first.astype(I32)


def _mix_kernel(x_ref, ya_ref, yb_ref, woa_ref, wob_ref, g_ref, wrt_ref, brt_ref,
                x1_ref, cls_ref, cnt_ref, buf_ref, sem_ref, *, n_steps, n_groups, n_experts):
    x1 = (x_ref[...] + jnp.dot(ya_ref[...], woa_ref[...], preferred_element_type=F32)
          + jnp.dot(yb_ref[...], wob_ref[...], preferred_element_type=F32))

    def fill(slot):
        buf_ref[slot] = x1
    _emit_rows(buf_ref, x1_ref, sem_ref, n_steps, fill)
    h = _rms(x1, g_ref[...]).astype(BF16)
    logits = lax.dot_general(wrt_ref[...], h, (((1,), (1,)), ((), ())), preferred_element_type=F32) + brt_ref[...]
    row = lax.broadcasted_iota(I32, logits.shape, 0)
    g_sel = _first_row_of_max(jnp.where(row < n_groups, logits, NEG), row)
    e_lo = n_groups + g_sel * n_experts
    e_logits = jnp.where((row >= e_lo) & (row < e_lo + n_experts), logits, NEG)
    top1 = _first_row_of_max(e_logits, row)
    top2 = _first_row_of_max(jnp.where(row == top1, NEG, e_logits), row)
    ea = jnp.minimum(top1, top2) - e_lo
    eb = jnp.maximum(top1, top2) - e_lo
    pair = ((ea * (2 * n_experts - 1 - ea)) >> 1) + (eb - ea - 1)
    n_pairs = n_experts * (n_experts - 1) // 2
    cls = g_sel * n_pairs + pair
    cls_ref[0] = cls

    @pl.when(pl.program_id(0) == 0)
    def _zero():
        cnt_ref[...] = jnp.zeros(cnt_ref.shape, F32)

    cnt_ref[...] += jnp.sum(jnp.where(row == cls, 1.0, 0.0), axis=1, keepdims=True)


def _mix(x2d, ya, yb, wo_a, wo_b, gain, wrt, brt, *, n_groups, n_experts):
    t, d = x2d.shape[0], gain.shape[1]
    ts = min(TOKEN_TILE, t)
    row = lambda i: (i, 0)
    const = lambda i: (0, 0)
    x1_shape = _row_tiled(t, d)
    return pl.pallas_call(
        functools.partial(_mix_kernel, n_steps=t // ts, n_groups=n_groups, n_experts=n_experts),
        grid=(t // ts,),
        in_specs=[pl.BlockSpec((ts, d), row), pl.BlockSpec((ts, ya.shape[1]), row),
                  pl.BlockSpec((ts, yb.shape[1]), row), pl.BlockSpec(wo_a.shape, const),
                  pl.BlockSpec(wo_b.shape, const), pl.BlockSpec((1, d), const),
                  pl.BlockSpec(wrt.shape, const), pl.BlockSpec(brt.shape, const)],
        out_specs=[pl.BlockSpec(memory_space=pl.ANY), pl.BlockSpec((1, 1, ts), lambda i: (i, 0, 0)),
                   pl.BlockSpec((ROUTER_LANES, 1), const)],
        out_shape=[jax.ShapeDtypeStruct(x1_shape, F32), jax.ShapeDtypeStruct((t // ts, 1, ts), I32),
                   jax.ShapeDtypeStruct((ROUTER_LANES, 1), F32)],
        scratch_shapes=[pltpu.VMEM((2, ts, d), F32), pltpu.SemaphoreType.DMA((2,))],
        compiler_params=_cparams("arbitrary"),
        name="mix",
    )(x2d, ya, yb, wo_a, wo_b, gain, wrt, brt)


ISSUE_UNROLL = 8
DMA_LANES = 2


def _row_copy(src, src_row, dst, dst_row, sem):
    return pltpu.make_async_copy(src.at[src_row], dst.at[dst_row], sem)


def _rows_copy(src, dst, n, sem):
    return pltpu.make_async_copy(src.at[pl.ds(0, n)], dst.at[pl.ds(0, n)], sem)


def _slot_kernel(cls_ref, start_ref, pos_ref, next_ref, earlier_ref):
    ts = cls_ref.shape[2]

    @pl.when(pl.program_id(0) == 0)
    def _init():
        next_ref[...] = start_ref[...]
        before = lax.broadcasted_iota(I32, (ts, ts), 0) < lax.broadcasted_iota(I32, (ts, ts), 1)
        earlier_ref[...] = jnp.where(before, 1.0, 0.0).astype(BF16)

    member = lax.broadcasted_iota(I32, (ROUTER_LANES, ts), 0) == cls_ref[0]
    onehot = jnp.where(member, 1.0, 0.0)
    ahead = jnp.dot(onehot.astype(BF16), earlier_ref[...], preferred_element_type=F32)
    pos_ref[0] = jnp.sum(onehot * (ahead + next_ref[...]), axis=0, keepdims=True).astype(I32)
    next_ref[...] += jnp.sum(onehot, axis=1, keepdims=True)


def _slots(cls, slot_start):
    n_tiles, _, ts = cls.shape
    blk = pl.BlockSpec((1, 1, ts), lambda i: (i, 0, 0))
    return pl.pallas_call(
        _slot_kernel,
        grid=(n_tiles,),
        in_specs=[blk, pl.BlockSpec((ROUTER_LANES, 1), lambda i: (0, 0))],
        out_specs=blk,
        out_shape=jax.ShapeDtypeStruct(cls.shape, I32),
        scratch_shapes=[pltpu.VMEM((ROUTER_LANES, 1), F32), pltpu.VMEM((ts, ts), BF16)],
        compiler_params=_cparams("arbitrary"),
        name="slots",
    )(cls, slot_start.astype(F32).reshape(ROUTER_LANES, 1))


def _dispatch_kernel(fill_ref, end_ref, pos_ref, x_ref, xs_ref, zero_ref, sem, *, n_cls):
    step = pl.program_id(0)
    ts = x_ref.shape[0]

    @pl.when(step == 0)
    def _init():
        zero_ref[...] = jnp.zeros(zero_ref.shape, F32)

    def place(j, carry):
        for lane in range(DMA_LANES):
            r = j * DMA_LANES + lane
            _row_copy(x_ref, r, xs_ref, pos_ref[0, 0, r], sem).start(priority=lane)
        return carry
    lax.fori_loop(0, ts // DMA_LANES, place, 0, unroll=ISSUE_UNROLL // DMA_LANES)
    _rows_copy(x_ref, xs_ref, ts, sem).wait()

    @pl.when(step == pl.num_programs(0) - 1)
    def _pad():
        def per_class(c, carry):
            lo, hi = fill_ref[c], end_ref[c]

            def fill(slot, inner):
                _row_copy(zero_ref, 0, xs_ref, slot, sem).start()
                return inner
            lax.fori_loop(lo, hi, fill, 0)

            def fill_wait(slot, inner):
                _row_copy(zero_ref, 0, xs_ref, 0, sem).wait()
                return inner
            lax.fori_loop(lo, hi, fill_wait, 0)
            return carry
        lax.fori_loop(0, n_cls, per_class, 0)

        tile = zero_ref.shape[0]
        first_free = lax.div(end_ref[n_cls - 1], tile)
        n_tiles = xs_ref.shape[0] // tile

        def tile_copy(ti):
            return pltpu.make_async_copy(zero_ref, xs_ref.at[pl.ds(pl.multiple_of(ti * tile, tile), tile)], sem)

        def clear(ti, carry):
            tile_copy(ti).start()
            return carry
        lax.fori_loop(first_free, n_tiles, clear, 0)

        def clear_wait(ti, carry):
            tile_copy(ti).wait()
            return carry
        lax.fori_loop(first_free, n_tiles, clear_wait, 0)


def _dispatch(x1, pos, fill_from, slot_end, n_slots, *, n_cls, tile):
    n_tiles, _, ts = pos.shape
    return pl.pallas_call(
        functools.partial(_dispatch_kernel, n_cls=n_cls),
        grid_spec=pltpu.PrefetchScalarGridSpec(
            num_scalar_prefetch=2,
            grid=(n_tiles,),
            in_specs=[pl.BlockSpec((1, 1, ts), lambda i, *_: (i, 0, 0), memory_space=pltpu.SMEM),
                      _rows_spec(ts, x1.shape)],
            out_specs=pl.BlockSpec(memory_space=pl.ANY),
            scratch_shapes=[pltpu.VMEM((tile, *x1.shape[1:]), F32), pltpu.SemaphoreType.DMA],
        ),
        out_shape=jax.ShapeDtypeStruct((n_slots, *x1.shape[1:]), F32),
        compiler_params=_cparams("arbitrary"),
        name="dispatch",
    )(fill_from, slot_end, pos, x1)


def _gather_rows(pos_ref, ys_ref, rows_ref, sem):
    ts = rows_ref.shape[0]

    def fetch(j, carry):
        for lane in range(DMA_LANES):
            r = j * DMA_LANES + lane
            _row_copy(ys_ref, pos_ref[0, 0, r], rows_ref, r, sem).start(priority=lane)
        return carry
    lax.fori_loop(0, ts // DMA_LANES, fetch, 0, unroll=ISSUE_UNROLL // DMA_LANES)
    _rows_copy(ys_ref, rows_ref, ts, sem).wait()


def _unpermute_kernel(pos_ref, ys_ref, o_ref, stage_ref, sem, out_sem, *, n_steps):
    def fill(slot):
        _gather_rows(pos_ref, ys_ref, stage_ref.at[slot], sem)
    _emit_rows(stage_ref, o_ref, out_sem, n_steps, fill)


def _unpermute(ys, pos, t):
    n_tiles, _, ts = pos.shape
    hbm = pl.BlockSpec(memory_space=pl.ANY)
    return pl.pallas_call(
        functools.partial(_unpermute_kernel, n_steps=n_tiles),
        grid=(n_tiles,),
        in_specs=[pl.BlockSpec((1, 1, ts), lambda i: (i, 0, 0), memory_space=pltpu.SMEM), hbm],
        out_specs=hbm,
        out_shape=jax.ShapeDtypeStruct((t, ys.shape[1] * ys.shape[2]), F32),
        scratch_shapes=[pltpu.VMEM((2, ts, *ys.shape[1:]), F32), pltpu.SemaphoreType.DMA,
                        pltpu.SemaphoreType.DMA((2,))],
        compiler_params=_cparams("arbitrary"),
        name="unpermute",
    )(pos, ys)


def _expert_kernel(tg_ref, ta_ref, tb_ref, nv_ref, xs_ref, g_ref, wr_ref, br_ref,
                   wga_ref, wua_ref, wda_ref, wgb_ref, wub_ref, wdb_ref, ys_ref,
                   xbuf_ref, in_sem_ref, buf_ref, sem_ref, *, n_steps, n_groups, n_experts):
    i = pl.program_id(0)
    in_slot = _fetch_rows(xs_ref, xbuf_ref, in_sem_ref, n_steps)

    def run():
        x = xbuf_ref[in_slot]
        h = _rms(x, g_ref[...]).astype(BF16)
        logits = _router_logits(h, wr_ref, br_ref)
        gates = [jnp.dot(h, w[0, 0], preferred_element_type=F32) for w in (wga_ref, wgb_ref)]
        ups = [jnp.dot(h, w[0, 0], preferred_element_type=F32) for w in (wua_ref, wub_ref)]
        lane = lax.broadcasted_iota(I32, logits.shape, 1)
        grp, ea, eb = tg_ref[i], ta_ref[i], tb_ref[i]

        def pick(idx):
            return jnp.sum(jnp.where(lane == idx, logits, 0.0), axis=1, keepdims=True)

        g_logits = jnp.where(lane < n_groups, logits, NEG)
        g_max = jnp.max(g_logits, axis=1, keepdims=True)
        g_w = jnp.exp(pick(grp) - g_max) / jnp.sum(jnp.exp(g_logits - g_max), axis=1, keepdims=True)
        e_lo = n_groups + grp * n_experts
        w_a = jax.nn.sigmoid(pick(e_lo + ea) - pick(e_lo + eb))
        acts = [(gate * jax.nn.sigmoid(gate) * up * comb).astype(BF16)
                for gate, up, comb in zip(gates, ups, (g_w * w_a, g_w * (1.0 - w_a)))]
        return (x + jnp.dot(acts[0], wda_ref[0, 0], preferred_element_type=F32)
                + jnp.dot(acts[1], wdb_ref[0, 0], preferred_element_type=F32))

    def fill(slot):
        @pl.when(nv_ref[i] == 0)
        def _unused_tile():
            buf_ref[slot] = jnp.zeros(buf_ref.shape[1:], F32)

        @pl.when(nv_ref[i] > 0)
        def _used_tile():
            buf_ref[slot] = run()
    _emit_rows(buf_ref, ys_ref, sem_ref, n_steps, fill)


def _experts(xs, gain, wr, br, wg, wu, wd, meta, *, tile, n_groups, n_experts):
    n_slots, d = xs.shape[0], gain.shape[1]
    de = wg.shape[-1]
    n_steps = n_slots // tile
    tile_g, tile_a, tile_b, tile_nv = meta
    const = lambda i, *_: (0, 0)
    sel_a = lambda i, tg, ta, tb, nv: (tg[i], ta[i], 0, 0)
    sel_b = lambda i, tg, ta, tb, nv: (tg[i], tb[i], 0, 0)
    up_spec_a = pl.BlockSpec((1, 1, d, de), sel_a)
    up_spec_b = pl.BlockSpec((1, 1, d, de), sel_b)
    hbm = pl.BlockSpec(memory_space=pl.ANY)
    return pl.pallas_call(
        functools.partial(_expert_kernel, n_steps=n_steps, n_groups=n_groups, n_experts=n_experts),
        grid_spec=pltpu.PrefetchScalarGridSpec(
            num_scalar_prefetch=4,
            grid=(n_steps,),
            in_specs=[hbm, pl.BlockSpec((1, d), const),
                      pl.BlockSpec(wr.shape, const), pl.BlockSpec(br.shape, const),
                      up_spec_a, up_spec_a, pl.BlockSpec((1, 1, de, d), sel_a),
                      up_spec_b, up_spec_b, pl.BlockSpec((1, 1, de, d), sel_b)],
            out_specs=hbm,
            scratch_shapes=[pltpu.VMEM((2, tile, d), F32), pltpu.SemaphoreType.DMA((2,)),
                            pltpu.VMEM((2, tile, d), F32), pltpu.SemaphoreType.DMA((2,))],
        ),
        out_shape=jax.ShapeDtypeStruct(xs.shape, F32),
        compiler_params=_cparams("arbitrary"),
        name="experts",
    )(tile_g, tile_a, tile_b, tile_nv, xs, gain, wr, br, wg, wu, wd, wg, wu, wd)


def _class_layout(counts, n_cls, n_pairs, n_experts, n_tiles, tile):
    counts = counts[:n_cls, 0].astype(I32)
    tiles_c = (counts + tile - 1) // tile
    tile_end = jnp.cumsum(tiles_c)
    tile_start = tile_end - tiles_c
    n_active = tile_end[-1]
    pad = ROUTER_LANES - n_cls
    slot_start = jnp.pad(tile_start * tile, (0, pad))
    slot_end = jnp.pad(tile_end * tile, (0, pad))
    fill_from = jnp.pad(tile_start * tile + counts, (0, pad))
    t_idx = jnp.arange(n_tiles, dtype=I32)
    t_cls = jnp.minimum(jnp.sum((tile_end[None, :] <= t_idx[:, None]).astype(I32), axis=1), n_cls - 1)
    active = t_idx < n_active
    t_nv = jnp.where(active, jnp.clip(counts[t_cls] - (t_idx - tile_start[t_cls]) * tile, 0, tile), 0)
    t_cls = jnp.where(active, t_cls, t_cls[jnp.maximum(n_active - 1, 0)])
    pair_a, pair_b = np.triu_indices(n_experts, 1)
    pair = t_cls % n_pairs
    meta = (t_cls // n_pairs, jnp.asarray(pair_a, I32)[pair], jnp.asarray(pair_b, I32)[pair], t_nv.astype(I32))
    return slot_start.astype(I32), fill_from.astype(I32), slot_end.astype(I32), meta


def _rope_tables(pos, group, signed_half):
    half = group // 2
    lane = np.arange(LANE)
    inv = (ROPE_THETA ** (-jnp.arange(0, group, 2, dtype=F32) / group))[lane % half]
    ang = pos.astype(F32)[:, None] * inv[None, :]
    sign = np.where((lane % group) < signed_half, -1.0, 1.0).astype(np.float32)
    return jnp.cos(ang), jnp.sin(ang) * sign[None, :]


def _layer(x2d, b, s, pos, cache, state0, p, li, dims):
    aw, bw, hda, dkb = dims["aw"], dims["bw"], dims["hda"], dims["dkb"]
    n_groups, n_experts = dims["n_groups"], dims["n_experts"]
    lam_init = 0.8 - 0.6 * float(np.exp(-0.3 * li))
    t = b * s
    tabs = (*_rope_tables(pos, hda, hda // 2), *_rope_tables(pos, dkb, dkb // 2))
    qa, ka, va, qb, kb, vb, gb = _project(x2d, s, p["norm_attn"], p["w_in"], p["gm"], p["gq"], p["gk"], tabs,
                                          aw=aw, bw=bw, hda=hda, dkb=dkb)
    shape3 = lambda a: a.reshape(b, s, a.shape[-1])
    ya = _diff_attention(shape3(qa), shape3(ka), shape3(va), cache, p["lam_vec"], p["subln"],
                         causal=cache is None, hda=hda, lam_init=lam_init)
    yb, new_state = _retention(shape3(qb), shape3(kb), shape3(vb), shape3(gb), p["ret_norm"], state0)
    x1, cls, counts = _mix(x2d, ya.reshape(t, aw), yb.reshape(t, bw), p["wo_a"], p["wo_b"], p["norm_ffn"],
                           p["wrt"], p["brt"], n_groups=n_groups, n_experts=n_experts)
    n_pairs = n_experts * (n_experts - 1) // 2
    n_cls = n_groups * n_pairs
    tile = EXPERT_TILE if t >= EXPERT_TILE * n_cls else SMALL_EXPERT_TILE
    n_tiles = t // tile + n_cls
    slot_start, fill_from, slot_end, meta = _class_layout(counts, n_cls, n_pairs, n_experts, n_tiles, tile)
    slot_of = _slots(cls, slot_start)
    xs = _dispatch(x1, slot_of, fill_from, slot_end, n_tiles * tile, n_cls=n_cls, tile=tile)
    ys = _experts(xs, p["norm_ffn"], p["wr"], p["br"], p["w_gate"], p["w_up"], p["w_down"], meta,
                  tile=tile, n_groups=n_groups, n_experts=n_experts)
    return _unpermute(ys, slot_of, t), ka, va, new_state


def kernel(x_prompt, x_sample, cache_k, cache_v, state_ret, norm_attn, w_in, q_norm, k_norm, lam_vec,
           subln, ret_norm, w_out, norm_ffn, w_group, b_group, w_expert, b_expert, w_gate, w_up, w_down):
    depth, dec_b, past, heads2, hda = cache_k.shape
    _, _, n_ret, dkb, dvb = state_ret.shape
    n_groups, n_experts = w_gate.shape[1], w_gate.shape[2]
    aw, bw = heads2 * hda, n_ret * dkb
    assert 2 * hda == LANE and dkb == LANE and dvb == LANE and aw % LANE == 0
    assert n_groups * (1 + n_experts) <= ROUTER_LANES
    assert n_groups * n_experts * (n_experts - 1) // 2 <= ROUTER_LANES
    dims = dict(aw=aw, bw=bw, hda=hda, dkb=dkb, n_groups=n_groups, n_experts=n_experts)
    d = x_prompt.shape[-1]
    group_of = np.arange(aw) // hda
    gm = jnp.asarray((group_of[:, None] == group_of[None, :]) / hda, BF16)
    row = lambda v: v.reshape(1, -1).astype(F32)

    def layer_params(li):
        wr = jnp.concatenate([w_group[li], w_expert[li].reshape(d, n_groups * n_experts)], axis=1)
        br = jnp.concatenate([b_group[li], b_expert[li].reshape(-1)])
        lane_pad = ROUTER_LANES - wr.shape[1]
        return dict(
            norm_attn=row(norm_attn[li]), w_in=w_in[li].astype(BF16), gm=gm,
            gq=row(jnp.tile(q_norm[li], heads2)), gk=row(jnp.tile(k_norm[li], heads2)),
            lam_vec=lam_vec[li].astype(F32), subln=row(subln[li]), ret_norm=row(ret_norm[li]),
            wo_a=w_out[li, :aw].astype(BF16), wo_b=w_out[li, aw:].astype(BF16), norm_ffn=row(norm_ffn[li]),
            wr=jnp.pad(wr, ((0, 0), (0, lane_pad))).astype(BF16), br=row(jnp.pad(br, (0, lane_pad))),
            wrt=jnp.pad(wr, ((0, 0), (0, lane_pad))).T.astype(BF16),
            brt=jnp.pad(br, (0, lane_pad)).reshape(-1, 1).astype(F32),
            w_gate=w_gate[li].astype(BF16), w_up=w_up[li].astype(BF16), w_down=w_down[li].astype(BF16))

    pos_p = jnp.arange(x_prompt.shape[1])
    pos_s = past + jnp.arange(x_sample.shape[1])
    (bp, sp), (bs, ss) = x_prompt.shape[:2], x_sample.shape[:2]
    xp, xs = x_prompt.reshape(bp * sp, d), x_sample.reshape(bs * ss, d)
    outs = [[] for _ in range(6)]
    for li in range(depth):
        p = layer_params(li)
        xp, ka, va, st = _layer(xp, bp, sp, pos_p, None, None, p, li, dims)
        outs[0].append(ka.reshape(bp, sp, heads2, hda))
        outs[1].append(va.reshape(bp, sp, heads2 // 2, 2 * hda))
        outs[2].append(st)
        cache = (cache_k[li].reshape(dec_b, past, aw), cache_v[li].reshape(dec_b, past, aw))
        xs, ka, va, st = _layer(xs, bs, ss, pos_s, cache, state_ret[li], p, li, dims)
        outs[3].append(ka.reshape(bs, ss, heads2, hda))
        outs[4].append(va.reshape(bs, ss, heads2 // 2, 2 * hda))
        outs[5].append(st)
    return (xp.reshape(bp, sp, d), xs.reshape(bs, ss, d), *(jnp.stack(o) for o in outs))
```

```python
import functools

import jax
import jax.numpy as jnp
import numpy as np
from jax import lax
from jax.experimental import pallas as pl
from jax.experimental.pallas import tpu as pltpu

F32 = jnp.float32
BF16 = jnp.bfloat16
I32 = jnp.int32

CHUNK = 64
ROPE_THETA = 10000.0
EPS = 1e-6
NEG = -1e30
LANE = 128
SUBLANES = 8
ROUTER_LANES = 128
VMEM_LIMIT_BYTES = 52 * 1024 * 1024

TOKEN_TILE = 512
ATTN_TILE = 256
CACHE_TILE = 512
LOG2E = 1.4426950408889634
RET_CHUNK = 256
EXPERT_TILE = 256
SMALL_EXPERT_TILE = 16


def _cparams(*sem):
    return pltpu.CompilerParams(dimension_semantics=sem, vmem_limit_bytes=VMEM_LIMIT_BYTES)


def _rms(x, gain):
    ms = jnp.mean(x * x, axis=-1, keepdims=True)
    return x * lax.rsqrt(ms + EPS) * gain


def _rot_half(y, half):
    if 2 * half == LANE:
        return pltpu.roll(y, half, axis=1)
    lane = lax.broadcasted_iota(I32, y.shape, 1)
    first = (lane & (2 * half - 1)) < half
    return jnp.where(first, pltpu.roll(y, LANE - half, axis=1), pltpu.roll(y, half, axis=1))


def _row_tiled(t, d):
    assert d % (SUBLANES * LANE) == 0
    return (t, d // LANE, LANE)


def _rows_spec(ts, shape):
    zeros = (0,) * (len(shape) - 1)
    return pl.BlockSpec((ts, *shape[1:]), lambda i, *s: (i, *zeros))


def _retile_copies(buf_ref, slot, out_ref, row0, sem_ref):
    rows = buf_ref.shape[1]
    copies = []
    for c in range(out_ref.shape[1] if len(out_ref.shape) == 3 else buf_ref.shape[2]):
        lanes = pl.ds(c * LANE, LANE)
        if len(out_ref.shape) == 3:
            src, dst = buf_ref.at[slot, :, lanes], out_ref.at[pl.ds(row0, rows), c, :]
        else:
            src, dst = buf_ref.at[slot, :, c, :], out_ref.at[pl.ds(row0, rows), lanes]
        copies.append(pltpu.make_async_copy(src, dst, sem_ref.at[slot]))
    return copies


def _fetch_rows(src_ref, buf_ref, sem_ref, n_steps):
    step = pl.program_id(0)
    slot = lax.rem(step, 2)
    rows = buf_ref.shape[1]

    def copies(s, row0):
        return [pltpu.make_async_copy(src_ref.at[pl.ds(row0, rows), c, :],
                                      buf_ref.at[s, :, pl.ds(c * LANE, LANE)], sem_ref.at[s])
                for c in range(src_ref.shape[1])]

    @pl.when(step == 0)
    def _first():
        for cp in copies(0, 0):
            cp.start()

    @pl.when(step + 1 < n_steps)
    def _next():
        for cp in copies(1 - slot, pl.multiple_of((step + 1) * rows, rows)):
            cp.start()

    for cp in copies(slot, 0):
        cp.wait()
    return slot


def _emit_rows(buf_ref, out_ref, sem_ref, n_steps, fill):
    step = pl.program_id(0)
    slot = lax.rem(step, 2)
    rows = buf_ref.shape[1]

    def drain(s):
        for cp in _retile_copies(buf_ref, s, out_ref, 0, sem_ref):
            cp.wait()

    @pl.when(step >= 2)
    def _reuse():
        drain(slot)

    fill(slot)
    for cp in _retile_copies(buf_ref, slot, out_ref, pl.multiple_of(step * rows, rows), sem_ref):
        cp.start()

    @pl.when(step == n_steps - 1)
    def _finish():
        drain(slot)
        if n_steps >= 2:
            drain(1 - slot)


def _proj_kernel(x_ref, g_ref, w_ref, gm_ref, gq_ref, gk_ref, ca_ref, sa_ref, cb_ref, sb_ref,
                 qa_ref, ka_ref, va_ref, vabf_ref, qb_ref, kb_ref, vb_ref, gb_ref, vbuf_ref, vsem_ref,
                 *, n_steps, aw, bw, hda, dkb):
    h = _rms(x_ref[...], g_ref[...]).astype(BF16)

    def proj(c0, width):
        return jnp.dot(h, w_ref[:, c0:c0 + width], preferred_element_type=F32)

    def qk_norm_rot(z, gain_ref, out_ref):
        ss = jnp.dot((z * z).astype(BF16), gm_ref[...], preferred_element_type=F32)
        y = z * lax.rsqrt(ss + EPS) * gain_ref[...]
        for c in range(aw // LANE):
            yc = y[:, c * LANE:(c + 1) * LANE]
            r = yc * ca_ref[...] + _rot_half(yc, hda // 2) * sa_ref[...]
            out_ref[:, c * LANE:(c + 1) * LANE] = r.astype(out_ref.dtype)

    def rot_b(z, out_ref, scale):
        for c in range(bw // LANE):
            zc = z[:, c * LANE:(c + 1) * LANE]
            r = zc * cb_ref[...] + _rot_half(zc, dkb // 2) * sb_ref[...]
            out_ref[:, c * LANE:(c + 1) * LANE] = (r * scale).astype(out_ref.dtype)

    qk_norm_rot(proj(0, aw), gq_ref, qa_ref)
    qk_norm_rot(proj(aw, aw), gk_ref, ka_ref)
    va = proj(2 * aw, aw)
    vabf_ref[...] = va.astype(BF16)

    def fill(slot):
        vbuf_ref[slot] = va
    _emit_rows(vbuf_ref, va_ref, vsem_ref, n_steps, fill)
    rot_b(proj(3 * aw, bw), qb_ref, 1.0)
    rot_b(proj(3 * aw + bw, bw), kb_ref, dkb ** -0.5)
    vb_ref[...] = proj(3 * aw + 2 * bw, bw).astype(BF16)
    gb_ref[...] = proj(3 * aw + 3 * bw, bw).astype(BF16)


def _project(x2d, seq, gain, w_bf, gm, gq_t, gk_t, tabs, *, aw, bw, hda, dkb):
    t, d = x2d.shape[0], gain.shape[1]
    ts = min(TOKEN_TILE, seq)
    n_s = seq // ts
    row = lambda i: (i, 0)
    const = lambda i: (0, 0)
    tab = lambda i: (i % n_s, 0)
    tab_spec = pl.BlockSpec((ts, LANE), tab)
    out_a = pl.BlockSpec((ts, aw), row)
    out_b = pl.BlockSpec((ts, bw), row)
    return pl.pallas_call(
        functools.partial(_proj_kernel, n_steps=t // ts, aw=aw, bw=bw, hda=hda, dkb=dkb),
        grid=(t // ts,),
        in_specs=[pl.BlockSpec((ts, d), row), pl.BlockSpec((1, d), const),
                  pl.BlockSpec(w_bf.shape, const), pl.BlockSpec(gm.shape, const),
                  pl.BlockSpec((1, aw), const), pl.BlockSpec((1, aw), const),
                  tab_spec, tab_spec, tab_spec, tab_spec],
        out_specs=[out_a, out_a, pl.BlockSpec(memory_space=pl.ANY), out_a, out_b, out_b, out_b, out_b],
        out_shape=[jax.ShapeDtypeStruct((t, aw), BF16), jax.ShapeDtypeStruct((t, aw), F32),
                   jax.ShapeDtypeStruct((t, aw // LANE, LANE), F32), jax.ShapeDtypeStruct((t, aw), BF16),
                   jax.ShapeDtypeStruct((t, bw), BF16), jax.ShapeDtypeStruct((t, bw), BF16),
                   jax.ShapeDtypeStruct((t, bw), BF16), jax.ShapeDtypeStruct((t, bw), BF16)],
        scratch_shapes=[pltpu.VMEM((2, ts, aw), F32), pltpu.SemaphoreType.DMA((2,))],
        compiler_params=_cparams("arbitrary"),
        name="proj",
    )(x2d, gain, w_bf, gm, gq_t, gk_t, *tabs)


def _scaled_queries(q, hda):
    return (q.astype(F32) * (hda ** -0.5 * LOG2E)).astype(BF16)


def _split_halves(k, hda):
    lo = (lax.broadcasted_iota(I32, k.shape, 1) & (LANE - 1)) < hda
    return jnp.where(lo, k, 0.0).astype(BF16), jnp.where(lo, 0.0, k).astype(BF16)


def _reset_softmax_state(m_ref, l_ref, acc_ref):
    m_ref[...] = jnp.full(m_ref.shape, NEG, F32)
    l_ref[...] = jnp.zeros(l_ref.shape, F32)
    acc_ref[...] = jnp.zeros(acc_ref.shape, F32)


def _softmax_blocks(qs, blocks, m_ref, l_ref, acc_ref):
    scores = [[lax.dot_general(kk, q, (((1,), (1,)), ((), ())), preferred_element_type=F32)
               for q, halves in zip(qs, k_halves) for kk in halves] for k_halves, _, _ in blocks]
    for (_, vts, valid), block_scores in zip(blocks, scores):
        probs, alphas = [], []
        for c, s in enumerate(block_scores):
            if valid is not None:
                s = jnp.where(valid, s, NEG)
            m_old = m_ref[c]
            m_new = jnp.maximum(m_old, jnp.max(s, axis=0, keepdims=True))
            alpha = jnp.exp2(m_old - m_new)
            p = jnp.exp2(s - m_new)
            l_ref[c] = alpha * l_ref[c] + jnp.sum(p, axis=0, keepdims=True)
            m_ref[c] = m_new
            probs.append(p.astype(BF16))
            alphas.append(alpha)
        for c, (p, alpha) in enumerate(zip(probs, alphas)):
            acc_ref[c] = alpha * acc_ref[c] + jnp.dot(vts[c // 2], p, preferred_element_type=F32)


def _finish_heads(lv_ref, sg_ref, o_ref, l_ref, acc_ref, lam_init):
    lv = lv_ref[...]
    lam = (jnp.exp(jnp.sum(lv[0:1] * lv[1:2], axis=1, keepdims=True))
           - jnp.exp(jnp.sum(lv[2:3] * lv[3:4], axis=1, keepdims=True)) + lam_init)
    for h in range(o_ref.shape[2] // LANE):
        o_t = acc_ref[2 * h] / l_ref[2 * h] - lam * (acc_ref[2 * h + 1] / l_ref[2 * h + 1])
        o_ref[0, :, h * LANE:(h + 1) * LANE] = (_rms(o_t.T, sg_ref[...]) * (1.0 - lam_init)).astype(o_ref.dtype)


def _attn_prompt_kernel(q_ref, k_ref, v_ref, lv_ref, sg_ref, o_ref,
                        qs_ref, klo_ref, khi_ref, vt_ref, m_ref, l_ref, acc_ref, *, tile, hda, lam_init):
    qi = pl.program_id(1)
    n_pairs = q_ref.shape[2] // LANE
    n_blk = k_ref.shape[1] // tile

    @pl.when(qi == 0)
    def _stage_keys():
        def stage(j, carry):
            rows = pl.ds(pl.multiple_of(j * tile, tile), tile)
            klo_ref[rows, :], khi_ref[rows, :] = _split_halves(k_ref[0, rows, :], hda)
            v = v_ref[0, rows, :]
            for h in range(n_pairs):
                vt_ref[j * n_pairs + h] = v[:, h * LANE:(h + 1) * LANE].astype(F32).T.astype(BF16)
            return carry
        lax.fori_loop(0, n_blk, stage, 0)

    qs_ref[...] = _scaled_queries(q_ref[0], hda)
    _reset_softmax_state(m_ref, l_ref, acc_ref)

    cols = [slice(h * LANE, (h + 1) * LANE) for h in range(n_pairs)]

    def key_block(kb, valid):
        rows = pl.ds(pl.multiple_of(kb * tile, tile), tile)
        return ([(klo_ref[rows, c], khi_ref[rows, c]) for c in cols],
                [vt_ref[kb * n_pairs + h] for h in range(n_pairs)], valid)

    def attend(blocks):
        _softmax_blocks([qs_ref[:, c] for c in cols], blocks, m_ref, l_ref, acc_ref)

    def past_pair(j, carry):
        attend([key_block(2 * j, None), key_block(2 * j + 1, None)])
        return carry
    lax.fori_loop(0, qi // 2, past_pair, 0)
    shift = CHUNK.bit_length() - 1
    kchunk = lax.broadcasted_iota(I32, (tile, tile), 0) >> shift
    qchunk = lax.broadcasted_iota(I32, (tile, tile), 1) >> shift
    own = kchunk <= qchunk

    @pl.when(qi % 2 == 1)
    def _odd():
        attend([key_block(qi - 1, None), key_block(qi, own)])

    @pl.when(qi % 2 == 0)
    def _even():
        attend([key_block(qi, own)])
    _finish_heads(lv_ref, sg_ref, o_ref, l_ref, acc_ref, lam_init)


def _attn_cached_kernel(q_ref, kc_ref, vc_ref, kn_ref, vn_ref, lv_ref, sg_ref, o_ref,
                        qs_ref, m_ref, l_ref, acc_ref, *, hda, lam_init):
    pi = pl.program_id(1)
    n_pairs = q_ref.shape[2] // LANE

    @pl.when(pi == 0)
    def _start():
        qs_ref[...] = _scaled_queries(q_ref[0], hda)
        _reset_softmax_state(m_ref, l_ref, acc_ref)

    def key_block(k_src, v_src):
        k_lo, k_hi = _split_halves(k_src[0], hda)
        v = v_src[0]
        cols = [slice(h * LANE, (h + 1) * LANE) for h in range(n_pairs)]
        block = ([(k_lo[:, c], k_hi[:, c]) for c in cols], [v[:, c].astype(F32).T.astype(BF16) for c in cols], None)
        _softmax_blocks([qs_ref[:, c] for c in cols], [block], m_ref, l_ref, acc_ref)

    key_block(kc_ref, vc_ref)

    @pl.when(pi == pl.num_programs(1) - 1)
    def _finish():
        key_block(kn_ref, vn_ref)
        _finish_heads(lv_ref, sg_ref, o_ref, l_ref, acc_ref, lam_init)


def _diff_attention(qa, ka, va, cache, lam_vec, subln, *, causal, hda, lam_init):
    b, s, w = qa.shape
    n_chains = 2 * (w // LANE)
    const = lambda bi, si: (0, 0)
    whole = pl.BlockSpec((1, s, w), lambda bi, si: (bi, 0, 0))
    small = [pl.BlockSpec(lam_vec.shape, const), pl.BlockSpec((1, LANE), const)]

    def state(tq):
        return [pltpu.VMEM((tq, w), BF16), pltpu.VMEM((n_chains, 1, tq), F32),
                pltpu.VMEM((n_chains, 1, tq), F32), pltpu.VMEM((n_chains, LANE, tq), F32)]

    if causal:
        tile = min(ATTN_TILE, s)
        q_blk = pl.BlockSpec((1, tile, w), lambda bi, qi: (bi, qi, 0))
        qs, *softmax_state = state(tile)
        return pl.pallas_call(
            functools.partial(_attn_prompt_kernel, tile=tile, hda=hda, lam_init=lam_init),
            grid=(b, s // tile),
            in_specs=[q_blk, whole, whole] + small,
            out_specs=q_blk,
            out_shape=jax.ShapeDtypeStruct((b, s, w), BF16),
            scratch_shapes=[qs, pltpu.VMEM((s, w), BF16), pltpu.VMEM((s, w), BF16),
                            pltpu.VMEM((s // tile * (w // LANE), LANE, tile), BF16)] + softmax_state,
            compiler_params=_cparams("parallel", "arbitrary"),
            name="diff_attn",
        )(qa, ka, va, lam_vec, subln)
    ck, cv = cache
    tkc = min(CACHE_TILE, ck.shape[1])
    cache_blk = pl.BlockSpec((1, tkc, w), lambda bi, pi: (bi, pi, 0))
    return pl.pallas_call(
        functools.partial(_attn_cached_kernel, hda=hda, lam_init=lam_init),
        grid=(b, ck.shape[1] // tkc),
        in_specs=[whole, cache_blk, cache_blk, whole, whole] + small,
        out_specs=whole,
        out_shape=jax.ShapeDtypeStruct((b, s, w), BF16),
        scratch_shapes=state(s),
        compiler_params=_cparams("parallel", "arbitrary"),
        name="diff_attn_cached",
    )(qa, ck, cv, ka, va, lam_vec, subln)


def _retention_kernel(*refs, has_state, chunk):
    if has_state:
        q_ref, k_ref, v_ref, g_ref, rg_ref, s0_ref, y_ref, so_ref, st_ref, decay_ref = refs
    else:
        q_ref, k_ref, v_ref, g_ref, rg_ref, y_ref, so_ref, st_ref, decay_ref = refs
    ci = pl.program_id(1)
    n_heads = q_ref.shape[2] // LANE
    log_gammas = [float(np.log(1.0 - 2.0 ** (-5.0 - h))) for h in range(n_heads)]

    @pl.when(ci == 0)
    def _init():
        st_ref[...] = s0_ref[0] if has_state else jnp.zeros(st_ref.shape, F32)
        row = lax.broadcasted_iota(I32, (chunk, chunk), 0)
        col = lax.broadcasted_iota(I32, (chunk, chunk), 1)
        dist = (row - col).astype(F32)
        for h, lg in enumerate(log_gammas):
            decay_ref[h] = jnp.where(dist >= 0, jnp.exp(lg * jnp.maximum(dist, 0.0)), 0.0)

    idx = lax.broadcasted_iota(I32, (chunk, 1), 0).astype(F32)
    heads = list(enumerate(log_gammas))
    cols = [slice(h * LANE, (h + 1) * LANE) for h in range(n_heads)]
    qk = [lax.dot_general(q_ref[0, :, c], k_ref[0, :, c], (((1,), (1,)), ((), ())), preferred_element_type=F32)
          for c in cols]
    cross = [jnp.dot(q_ref[0, :, c], st_ref[h].astype(BF16), preferred_element_type=F32)
             for h, c in enumerate(cols)]
    kv = []
    for (h, lg), c in zip(heads, cols):
        kz = (k_ref[0, :, c].astype(F32) * jnp.exp(lg * (chunk - 1.0 - idx))).astype(BF16)
        kv.append(lax.dot_general(kz, v_ref[0, :, c], (((0,), (0,)), ((), ())), preferred_element_type=F32))
    intra = [jnp.dot((qk[h] * decay_ref[h]).astype(BF16), v_ref[0, :, c], preferred_element_type=F32)
             for h, c in enumerate(cols)]
    for (h, lg), c in zip(heads, cols):
        out = intra[h] + cross[h] * jnp.exp(lg * (idx + 1.0))
        st_ref[h] = float(np.exp(lg * chunk)) * st_ref[h] + kv[h]
        gate = g_ref[0, :, c].astype(F32)
        y_ref[0, :, c] = (_rms(out, rg_ref[...]) * (gate * jax.nn.sigmoid(gate))).astype(y_ref.dtype)

    @pl.when(ci == pl.num_programs(1) - 1)
    def _emit_state():
        so_ref[0] = st_ref[...]


def _retention(qb, kb, vb, gb, ret_g, state0):
    b, s, w = qb.shape
    n_heads = w // LANE
    chunk = min(RET_CHUNK, s)
    has_state = state0 is not None
    blk = pl.BlockSpec((1, chunk, w), lambda bi, ci: (bi, ci, 0))
    st_spec = pl.BlockSpec((1, n_heads, LANE, LANE), lambda bi, ci: (bi, 0, 0, 0))
    in_specs = [blk, blk, blk, blk, pl.BlockSpec((1, LANE), lambda bi, ci: (0, 0))]
    args = [qb, kb, vb, gb, ret_g]
    if has_state:
        in_specs.append(st_spec)
        args.append(state0)
    return pl.pallas_call(
        functools.partial(_retention_kernel, has_state=has_state, chunk=chunk),
        grid=(b, s // chunk),
        in_specs=in_specs,
        out_specs=[blk, st_spec],
        out_shape=[jax.ShapeDtypeStruct((b, s, w), BF16), jax.ShapeDtypeStruct((b, n_heads, LANE, LANE), F32)],
        scratch_shapes=[pltpu.VMEM((n_heads, LANE, LANE), F32), pltpu.VMEM((n_heads, chunk, chunk), F32)],
        compiler_params=_cparams("parallel", "arbitrary"),
        name="retention",
    )(*args)


def _router_logits(h_bf, wr_ref, br_ref):
    return jnp.dot(h_bf, wr_ref[...], preferred_element_type=F32) + br_ref[...]


def _first_row_of_max(vals, row):
    vmax = jnp.max(vals, axis=0, keepdims=True)
    first = jnp.min(jnp.where(vals == vmax, row.astype(F32), float(ROUTER_LANES)), axis=0, keepdims=True)
    return first.astype(I32)


def _mix_kernel(x_ref, ya_ref, yb_ref, woa_ref, wob_ref, g_ref, wrt_ref, brt_ref,
                x1_ref, cls_ref, cnt_ref, buf_ref, sem_ref, *, n_steps, n_groups, n_experts):
    x1 = (x_ref[...] + jnp.dot(ya_ref[...], woa_ref[...], preferred_element_type=F32)
          + jnp.dot(yb_ref[...], wob_ref[...], preferred_element_type=F32))

    def fill(slot):
        buf_ref[slot] = x1
    _emit_rows(buf_ref, x1_ref, sem_ref, n_steps, fill)
    h = _rms(x1, g_ref[...]).astype(BF16)
    logits = lax.dot_general(wrt_ref[...], h, (((1,), (1,)), ((), ())), preferred_element_type=F32) + brt_ref[...]
    row = lax.broadcasted_iota(I32, logits.shape, 0)
    g_sel = _first_row_of_max(jnp.where(row < n_groups, logits, NEG), row)
    e_lo = n_groups + g_sel * n_experts
    e_logits = jnp.where((row >= e_lo) & (row < e_lo + n_experts), logits, NEG)
    top1 = _first_row_of_max(e_logits, row)
    top2 = _first_row_of_max(jnp.where(row == top1, NEG, e_logits), row)
    ea = jnp.minimum(top1, top2) - e_lo
    eb = jnp.maximum(top1, top2) - e_lo
    pair = ((ea * (2 * n_experts - 1 - ea)) >> 1) + (eb - ea - 1)
    n_pairs = n_experts * (n_experts - 1) // 2
    cls = g_sel * n_pairs + pair
    cls_ref[0] = cls

    @pl.when(pl.program_id(0) == 0)
    def _zero():
        cnt_ref[...] = jnp.zeros(cnt_ref.shape, F32)

    cnt_ref[...] += jnp.sum(jnp.where(row == cls, 1.0, 0.0), axis=1, keepdims=True)


def _mix(x2d, ya, yb, wo_a, wo_b, gain, wrt, brt, *, n_groups, n_experts):
    t, d = x2d.shape[0], gain.shape[1]
    ts = min(TOKEN_TILE, t)
    row = lambda i: (i, 0)
    const = lambda i: (0, 0)
    x1_shape = _row_tiled(t, d)
    return pl.pallas_call(
        functools.partial(_mix_kernel, n_steps=t // ts, n_groups=n_groups, n_experts=n_experts),
        grid=(t // ts,),
        in_specs=[pl.BlockSpec((ts, d), row), pl.BlockSpec((ts, ya.shape[1]), row),
                  pl.BlockSpec((ts, yb.shape[1]), row), pl.BlockSpec(wo_a.shape, const),
                  pl.BlockSpec(wo_b.shape, const), pl.BlockSpec((1, d), const),
                  pl.BlockSpec(wrt.shape, const), pl.BlockSpec(brt.shape, const)],
        out_specs=[pl.BlockSpec(memory_space=pl.ANY), pl.BlockSpec((1, 1, ts), lambda i: (i, 0, 0)),
                   pl.BlockSpec((ROUTER_LANES, 1), const)],
        out_shape=[jax.ShapeDtypeStruct(x1_shape, F32), jax.ShapeDtypeStruct((t // ts, 1, ts), I32),
                   jax.ShapeDtypeStruct((ROUTER_LANES, 1), F32)],
        scratch_shapes=[pltpu.VMEM((2, ts, d), F32), pltpu.SemaphoreType.DMA((2,))],
        compiler_params=_cparams("arbitrary"),
        name="mix",
    )(x2d, ya, yb, wo_a, wo_b, gain, wrt, brt)


ISSUE_UNROLL = 8
DMA_LANES = 2


def _row_copy(src, src_row, dst, dst_row, sem):
    return pltpu.make_async_copy(src.at[src_row], dst.at[dst_row], sem)


def _rows_copy(src, dst, n, sem):
    return pltpu.make_async_copy(src.at[pl.ds(0, n)], dst.at[pl.ds(0, n)], sem)


def _slot_kernel(cls_ref, start_ref, pos_ref, next_ref, earlier_ref):
    ts = cls_ref.shape[2]

    @pl.when(pl.program_id(0) == 0)
    def _init():
        next_ref[...] = start_ref[...]
        before = lax.broadcasted_iota(I32, (ts, ts), 0) < lax.broadcasted_iota(I32, (ts, ts), 1)
        earlier_ref[...] = jnp.where(before, 1.0, 0.0).astype(BF16)

    member = lax.broadcasted_iota(I32, (ROUTER_LANES, ts), 0) == cls_ref[0]
    onehot = jnp.where(member, 1.0, 0.0)
    ahead = jnp.dot(onehot.astype(BF16), earlier_ref[...], preferred_element_type=F32)
    pos_ref[0] = jnp.sum(onehot * (ahead + next_ref[...]), axis=0, keepdims=True).astype(I32)
    next_ref[...] += jnp.sum(onehot, axis=1, keepdims=True)


def _slots(cls, slot_start):
    n_tiles, _, ts = cls.shape
    blk = pl.BlockSpec((1, 1, ts), lambda i: (i, 0, 0))
    return pl.pallas_call(
        _slot_kernel,
        grid=(n_tiles,),
        in_specs=[blk, pl.BlockSpec((ROUTER_LANES, 1), lambda i: (0, 0))],
        out_specs=blk,
        out_shape=jax.ShapeDtypeStruct(cls.shape, I32),
        scratch_shapes=[pltpu.VMEM((ROUTER_LANES, 1), F32), pltpu.VMEM((ts, ts), BF16)],
        compiler_params=_cparams("arbitrary"),
        name="slots",
    )(cls, slot_start.astype(F32).reshape(ROUTER_LANES, 1))


def _dispatch_kernel(fill_ref, end_ref, pos_ref, x_ref, xs_ref, zero_ref, sem, *, n_cls):
    step = pl.program_id(0)
    ts = x_ref.shape[0]

    @pl.when(step == 0)
    def _init():
        zero_ref[...] = jnp.zeros(zero_ref.shape, F32)

    def place(j, carry):
        for lane in range(DMA_LANES):
            r = j * DMA_LANES + lane
            _row_copy(x_ref, r, xs_ref, pos_ref[0, 0, r], sem).start(priority=lane)
        return carry
    lax.fori_loop(0, ts // DMA_LANES, place, 0, unroll=ISSUE_UNROLL // DMA_LANES)
    _rows_copy(x_ref, xs_ref, ts, sem).wait()

    @pl.when(step == pl.num_programs(0) - 1)
    def _pad():
        tile = zero_ref.shape[0]

        def pad_copies(c, act):
            row, n = fill_ref[c], end_ref[c] - fill_ref[c]
            p = tile // 2
            while p:
                piece = pltpu.make_async_copy(zero_ref.at[pl.ds(0, p)], xs_ref.at[pl.ds(row, p)], sem)
                pl.when((n & p) != 0)(functools.partial(act, piece))
                row = row + (n & p)
                p //= 2

        def start_class(c, carry):
            pad_copies(c, lambda piece: piece.start())
            return carry
        lax.fori_loop(0, n_cls, start_class, 0)

        def wait_class(c, carry):
            pad_copies(c, lambda piece: piece.wait())
            return carry
        lax.fori_loop(0, n_cls, wait_class, 0)

        first_free = lax.div(end_ref[n_cls - 1], tile)
        n_tiles = xs_ref.shape[0] // tile

        def tile_copy(ti):
            return pltpu.make_async_copy(zero_ref, xs_ref.at[pl.ds(pl.multiple_of(ti * tile, tile), tile)], sem)

        def clear(ti, carry):
            tile_copy(ti).start()
            return carry
        lax.fori_loop(first_free, n_tiles, clear, 0)

        def clear_wait(ti, carry):
            tile_copy(ti).wait()
            return carry
        lax.fori_loop(first_free, n_tiles, clear_wait, 0)


def _dispatch(x1, pos, fill_from, slot_end, n_slots, *, n_cls, tile):
    n_tiles, _, ts = pos.shape
    return pl.pallas_call(
        functools.partial(_dispatch_kernel, n_cls=n_cls),
        grid_spec=pltpu.PrefetchScalarGridSpec(
            num_scalar_prefetch=2,
            grid=(n_tiles,),
            in_specs=[pl.BlockSpec((1, 1, ts), lambda i, *_: (i, 0, 0), memory_space=pltpu.SMEM),
                      _rows_spec(ts, x1.shape)],
            out_specs=pl.BlockSpec(memory_space=pl.ANY),
            scratch_shapes=[pltpu.VMEM((tile, *x1.shape[1:]), F32), pltpu.SemaphoreType.DMA],
        ),
        out_shape=jax.ShapeDtypeStruct((n_slots, *x1.shape[1:]), F32),
        compiler_params=_cparams("arbitrary"),
        name="dispatch",
    )(fill_from, slot_end, pos, x1)


def _gather_rows(pos_ref, ys_ref, rows_ref, sem):
    ts = rows_ref.shape[0]

    def fetch(j, carry):
        for lane in range(DMA_LANES):
            r = j * DMA_LANES + lane
            _row_copy(ys_ref, pos_ref[0, 0, r], rows_ref, r, sem).start(priority=lane)
        return carry
    lax.fori_loop(0, ts // DMA_LANES, fetch, 0, unroll=ISSUE_UNROLL // DMA_LANES)
    _rows_copy(ys_ref, rows_ref, ts, sem).wait()


def _unpermute_kernel(pos_ref, ys_ref, o_ref, stage_ref, sem, out_sem, *, n_steps):
    def fill(slot):
        _gather_rows(pos_ref, ys_ref, stage_ref.at[slot], sem)
    _emit_rows(stage_ref, o_ref, out_sem, n_steps, fill)


def _unpermute(ys, pos, t):
    n_tiles, _, ts = pos.shape
    hbm = pl.BlockSpec(memory_space=pl.ANY)
    return pl.pallas_call(
        functools.partial(_unpermute_kernel, n_steps=n_tiles),
        grid=(n_tiles,),
        in_specs=[pl.BlockSpec((1, 1, ts), lambda i: (i, 0, 0), memory_space=pltpu.SMEM), hbm],
        out_specs=hbm,
        out_shape=jax.ShapeDtypeStruct((t, ys.shape[1] * ys.shape[2]), F32),
        scratch_shapes=[pltpu.VMEM((2, ts, *ys.shape[1:]), F32), pltpu.SemaphoreType.DMA,
                        pltpu.SemaphoreType.DMA((2,))],
        compiler_params=_cparams("arbitrary"),
        name="unpermute",
    )(pos, ys)


def _expert_kernel(tg_ref, ta_ref, tb_ref, nv_ref, xs_ref, g_ref, wr_ref, br_ref,
                   wga_ref, wua_ref, wda_ref, wgb_ref, wub_ref, wdb_ref, ys_ref,
                   xbuf_ref, in_sem_ref, buf_ref, sem_ref, *, n_steps, n_groups, n_experts):
    i = pl.program_id(0)
    in_slot = _fetch_rows(xs_ref, xbuf_ref, in_sem_ref, n_steps)

    def run():
        x = xbuf_ref[in_slot]
        h = _rms(x, g_ref[...]).astype(BF16)
        logits = _router_logits(h, wr_ref, br_ref)
        gates = [jnp.dot(h, w[0, 0], preferred_element_type=F32) for w in (wga_ref, wgb_ref)]
        ups = [jnp.dot(h, w[0, 0], preferred_element_type=F32) for w in (wua_ref, wub_ref)]
        lane = lax.broadcasted_iota(I32, logits.shape, 1)
        grp, ea, eb = tg_ref[i], ta_ref[i], tb_ref[i]

        def pick(idx):
            return jnp.sum(jnp.where(lane == idx, logits, 0.0), axis=1, keepdims=True)

        g_logits = jnp.where(lane < n_groups, logits, NEG)
        g_max = jnp.max(g_logits, axis=1, keepdims=True)
        g_w = jnp.exp(pick(grp) - g_max) / jnp.sum(jnp.exp(g_logits - g_max), axis=1, keepdims=True)
        e_lo = n_groups + grp * n_experts
        w_a = jax.nn.sigmoid(pick(e_lo + ea) - pick(e_lo + eb))
        acts = [(gate * jax.nn.sigmoid(gate) * up * comb).astype(BF16)
                for gate, up, comb in zip(gates, ups, (g_w * w_a, g_w * (1.0 - w_a)))]
        return (x + jnp.dot(acts[0], wda_ref[0, 0], preferred_element_type=F32)
                + jnp.dot(acts[1], wdb_ref[0, 0], preferred_element_type=F32))

    def fill(slot):
        @pl.when(nv_ref[i] == 0)
        def _unused_tile():
            buf_ref[slot] = jnp.zeros(buf_ref.shape[1:], F32)

        @pl.when(nv_ref[i] > 0)
        def _used_tile():
            buf_ref[slot] = run()
    _emit_rows(buf_ref, ys_ref, sem_ref, n_steps, fill)


def _experts(xs, gain, wr, br, wg, wu, wd, meta, *, tile, n_groups, n_experts):
    n_slots, d = xs.shape[0], gain.shape[1]
    de = wg.shape[-1]
    n_steps = n_slots // tile
    tile_g, tile_a, tile_b, tile_nv = meta
    const = lambda i, *_: (0, 0)
    sel_a = lambda i, tg, ta, tb, nv: (tg[i], ta[i], 0, 0)
    sel_b = lambda i, tg, ta, tb, nv: (tg[i], tb[i], 0, 0)
    up_spec_a = pl.BlockSpec((1, 1, d, de), sel_a)
    up_spec_b = pl.BlockSpec((1, 1, d, de), sel_b)
    hbm = pl.BlockSpec(memory_space=pl.ANY)
    return pl.pallas_call(
        functools.partial(_expert_kernel, n_steps=n_steps, n_groups=n_groups, n_experts=n_experts),
        grid_spec=pltpu.PrefetchScalarGridSpec(
            num_scalar_prefetch=4,
            grid=(n_steps,),
            in_specs=[hbm, pl.BlockSpec((1, d), const),
                      pl.BlockSpec(wr.shape, const), pl.BlockSpec(br.shape, const),
                      up_spec_a, up_spec_a, pl.BlockSpec((1, 1, de, d), sel_a),
                      up_spec_b, up_spec_b, pl.BlockSpec((1, 1, de, d), sel_b)],
            out_specs=hbm,
            scratch_shapes=[pltpu.VMEM((2, tile, d), F32), pltpu.SemaphoreType.DMA((2,)),
                            pltpu.VMEM((2, tile, d), F32), pltpu.SemaphoreType.DMA((2,))],
        ),
        out_shape=jax.ShapeDtypeStruct(xs.shape, F32),
        compiler_params=_cparams("arbitrary"),
        name="experts",
    )(tile_g, tile_a, tile_b, tile_nv, xs, gain, wr, br, wg, wu, wd, wg, wu, wd)


def _class_layout(counts, n_cls, n_pairs, n_experts, n_tiles, tile):
    counts = counts[:n_cls, 0].astype(I32)
    tiles_c = (counts + tile - 1) // tile
    tile_end = jnp.cumsum(tiles_c)
    tile_start = tile_end - tiles_c
    n_active = tile_end[-1]
    pad = ROUTER_LANES - n_cls
    slot_start = jnp.pad(tile_start * tile, (0, pad))
    slot_end = jnp.pad(tile_end * tile, (0, pad))
    fill_from = jnp.pad(tile_start * tile + counts, (0, pad))
    t_idx = jnp.arange(n_tiles, dtype=I32)
    t_cls = jnp.minimum(jnp.sum((tile_end[None, :] <= t_idx[:, None]).astype(I32), axis=1), n_cls - 1)
    active = t_idx < n_active
    t_nv = jnp.where(active, jnp.clip(counts[t_cls] - (t_idx - tile_start[t_cls]) * tile, 0, tile), 0)
    t_cls = jnp.where(active, t_cls, t_cls[jnp.maximum(n_active - 1, 0)])
    pair_a, pair_b = np.triu_indices(n_experts, 1)
    pair = t_cls % n_pairs
    meta = (t_cls // n_pairs, jnp.asarray(pair_a, I32)[pair], jnp.asarray(pair_b, I32)[pair], t_nv.astype(I32))
    return slot_start.astype(I32), fill_from.astype(I32), slot_end.astype(I32), meta


def _rope_tables(pos, group, signed_half):
    half = group // 2
    lane = np.arange(LANE)
    inv = (ROPE_THETA ** (-jnp.arange(0, group, 2, dtype=F32) / group))[lane % half]
    ang = pos.astype(F32)[:, None] * inv[None, :]
    sign = np.where((lane % group) < signed_half, -1.0, 1.0).astype(np.float32)
    return jnp.cos(ang), jnp.sin(ang) * sign[None, :]


def _layer(x2d, b, s, pos, cache, state0, p, li, dims):
    aw, bw, hda, dkb = dims["aw"], dims["bw"], dims["hda"], dims["dkb"]
    n_groups, n_experts = dims["n_groups"], dims["n_experts"]
    lam_init = 0.8 - 0.6 * float(np.exp(-0.3 * li))
    t = b * s
    tabs = (*_rope_tables(pos, hda, hda // 2), *_rope_tables(pos, dkb, dkb // 2))
    qa, ka, va, va_bf, qb, kb, vb, gb = _project(x2d, s, p["norm_attn"], p["w_in"], p["gm"], p["gq"], p["gk"], tabs,
                                                 aw=aw, bw=bw, hda=hda, dkb=dkb)
    shape3 = lambda a: a.reshape(b, s, a.shape[-1])
    ya = _diff_attention(shape3(qa), shape3(ka), shape3(va_bf), cache, p["lam_vec"], p["subln"],
                         causal=cache is None, hda=hda, lam_init=lam_init)
    yb, new_state = _retention(shape3(qb), shape3(kb), shape3(vb), shape3(gb), p["ret_norm"], state0)
    x1, cls, counts = _mix(x2d, ya.reshape(t, aw), yb.reshape(t, bw), p["wo_a"], p["wo_b"], p["norm_ffn"],
                           p["wrt"], p["brt"], n_groups=n_groups, n_experts=n_experts)
    n_pairs = n_experts * (n_experts - 1) // 2
    n_cls = n_groups * n_pairs
    tile = EXPERT_TILE if t >= EXPERT_TILE * n_cls else SMALL_EXPERT_TILE
    n_tiles = t // tile + n_cls
    slot_start, fill_from, slot_end, meta = _class_layout(counts, n_cls, n_pairs, n_experts, n_tiles, tile)
    slot_of = _slots(cls, slot_start)
    xs = _dispatch(x1, slot_of, fill_from, slot_end, n_tiles * tile, n_cls=n_cls, tile=tile)
    ys = _experts(xs, p["norm_ffn"], p["wr"], p["br"], p["w_gate"], p["w_up"], p["w_down"], meta,
                  tile=tile, n_groups=n_groups, n_experts=n_experts)
    return _unpermute(ys, slot_of, t), ka, va, new_state


def kernel(x_prompt, x_sample, cache_k, cache_v, state_ret, norm_attn, w_in, q_norm, k_norm, lam_vec,
           subln, ret_norm, w_out, norm_ffn, w_group, b_group, w_expert, b_expert, w_gate, w_up, w_down):
    depth, dec_b, past, heads2, hda = cache_k.shape
    _, _, n_ret, dkb, dvb = state_ret.shape
    n_groups, n_experts = w_gate.shape[1], w_gate.shape[2]
    aw, bw = heads2 * hda, n_ret * dkb
    assert 2 * hda == LANE and dkb == LANE and dvb == LANE and aw % LANE == 0
    assert n_groups * (1 + n_experts) <= ROUTER_LANES
    assert n_groups * n_experts * (n_experts - 1) // 2 <= ROUTER_LANES
    dims = dict(aw=aw, bw=bw, hda=hda, dkb=dkb, n_groups=n_groups, n_experts=n_experts)
    d = x_prompt.shape[-1]
    group_of = np.arange(aw) // hda
    gm = jnp.asarray((group_of[:, None] == group_of[None, :]) / hda, BF16)
    row = lambda v: v.reshape(1, -1).astype(F32)

    def layer_params(li):
        wr = jnp.concatenate([w_group[li], w_expert[li].reshape(d, n_groups * n_experts)], axis=1)
        br = jnp.concatenate([b_group[li], b_expert[li].reshape(-1)])
        lane_pad = ROUTER_LANES - wr.shape[1]
        return dict(
            norm_attn=row(norm_attn[li]), w_in=w_in[li].astype(BF16), gm=gm,
            gq=row(jnp.tile(q_norm[li], heads2)), gk=row(jnp.tile(k_norm[li], heads2)),
            lam_vec=lam_vec[li].astype(F32), subln=row(subln[li]), ret_norm=row(ret_norm[li]),
            wo_a=w_out[li, :aw].astype(BF16), wo_b=w_out[li, aw:].astype(BF16), norm_ffn=row(norm_ffn[li]),
            wr=jnp.pad(wr, ((0, 0), (0, lane_pad))).astype(BF16), br=row(jnp.pad(br, (0, lane_pad))),
            wrt=jnp.pad(wr, ((0, 0), (0, lane_pad))).T.astype(BF16),
            brt=jnp.pad(br, (0, lane_pad)).reshape(-1, 1).astype(F32),
            w_gate=w_gate[li].astype(BF16), w_up=w_up[li].astype(BF16), w_down=w_down[li].astype(BF16))

    pos_p = jnp.arange(x_prompt.shape[1])
    pos_s = past + jnp.arange(x_sample.shape[1])
    (bp, sp), (bs, ss) = x_prompt.shape[:2], x_sample.shape[:2]
    xp, xs = x_prompt.reshape(bp * sp, d), x_sample.reshape(bs * ss, d)
    outs = [[] for _ in range(6)]
    for li in range(depth):
        p = layer_params(li)
        xp, ka, va, st = _layer(xp, bp, sp, pos_p, None, None, p, li, dims)
        outs[0].append(ka.reshape(bp, sp, heads2, hda))
        outs[1].append(va.reshape(bp, sp, heads2 // 2, 2 * hda))
        outs[2].append(st)
        cache = (cache_k[li].reshape(dec_b, past, aw), cache_v[li].reshape(dec_b, past, aw))
        xs, ka, va, st = _layer(xs, bs, ss, pos_s, cache, state_ret[li], p, li, dims)
        outs[3].append(ka.reshape(bs, ss, heads2, hda))
        outs[4].append(va.reshape(bs, ss, heads2 // 2, 2 * hda))
        outs[5].append(st)
    return (xp.reshape(bp, sp, d), xs.reshape(bs, ss, d), *(jnp.stack(o) for o in outs))
```

```python
import functools

import jax
import jax.numpy as jnp
import numpy as np
from jax import lax
from jax.experimental import pallas as pl
from jax.experimental.pallas import tpu as pltpu

F32 = jnp.float32
BF16 = jnp.bfloat16
I32 = jnp.int32

CHUNK = 64
ROPE_THETA = 10000.0
EPS = 1e-6
NEG = -1e30
LANE = 128
SUBLANES = 8
ROUTER_LANES = 128
VMEM_LIMIT_BYTES = 52 * 1024 * 1024

TOKEN_TILE = 512
ATTN_TILE = 256
CACHE_TILE = 512
LOG2E = 1.4426950408889634
RET_CHUNK = 256
EXPERT_TILE = 256
SMALL_EXPERT_TILE = 16


def _cparams(*sem):
    return pltpu.CompilerParams(dimension_semantics=sem, vmem_limit_bytes=VMEM_LIMIT_BYTES)


def _rms(x, gain):
    ms = jnp.mean(x * x, axis=-1, keepdims=True)
    return x * lax.rsqrt(ms + EPS) * gain


def _rot_half(y, half):
    if 2 * half == LANE:
        return pltpu.roll(y, half, axis=1)
    lane = lax.broadcasted_iota(I32, y.shape, 1)
    first = (lane & (2 * half - 1)) < half
    return jnp.where(first, pltpu.roll(y, LANE - half, axis=1), pltpu.roll(y, half, axis=1))


def _row_tiled(t, d):
    assert d % (SUBLANES * LANE) == 0
    return (t, d // LANE, LANE)


def _rows_spec(ts, shape):
    zeros = (0,) * (len(shape) - 1)
    return pl.BlockSpec((ts, *shape[1:]), lambda i, *s: (i, *zeros))


def _retile_copies(buf_ref, slot, out_ref, row0, sem_ref):
    rows = buf_ref.shape[1]
    if len(buf_ref.shape) == 3 and len(out_ref.shape) == 2:
        return [pltpu.make_async_copy(buf_ref.at[slot], out_ref.at[pl.ds(row0, rows)], sem_ref.at[slot])]
    copies = []
    for c in range(out_ref.shape[1] if len(out_ref.shape) == 3 else buf_ref.shape[2]):
        lanes = pl.ds(c * LANE, LANE)
        if len(out_ref.shape) == 3:
            src, dst = buf_ref.at[slot, :, lanes], out_ref.at[pl.ds(row0, rows), c, :]
        else:
            src, dst = buf_ref.at[slot, :, c, :], out_ref.at[pl.ds(row0, rows), lanes]
        copies.append(pltpu.make_async_copy(src, dst, sem_ref.at[slot]))
    return copies


def _fetch_rows(src_ref, buf_ref, sem_ref, n_steps):
    step = pl.program_id(0)
    slot = lax.rem(step, 2)
    rows = buf_ref.shape[1]

    def copies(s, row0):
        return [pltpu.make_async_copy(src_ref.at[pl.ds(row0, rows), c, :],
                                      buf_ref.at[s, :, pl.ds(c * LANE, LANE)], sem_ref.at[s])
                for c in range(src_ref.shape[1])]

    @pl.when(step == 0)
    def _first():
        for cp in copies(0, 0):
            cp.start()

    @pl.when(step + 1 < n_steps)
    def _next():
        for cp in copies(1 - slot, pl.multiple_of((step + 1) * rows, rows)):
            cp.start()

    for cp in copies(slot, 0):
        cp.wait()
    return slot


def _emit_rows(buf_ref, out_ref, sem_ref, n_steps, fill):
    step = pl.program_id(0)
    slot = lax.rem(step, 2)
    rows = buf_ref.shape[1]

    def drain(s):
        for cp in _retile_copies(buf_ref, s, out_ref, 0, sem_ref):
            cp.wait()

    @pl.when(step >= 2)
    def _reuse():
        drain(slot)

    fill(slot)
    for cp in _retile_copies(buf_ref, slot, out_ref, pl.multiple_of(step * rows, rows), sem_ref):
        cp.start()

    @pl.when(step == n_steps - 1)
    def _finish():
        drain(slot)
        if n_steps >= 2:
            drain(1 - slot)


def _proj_kernel(*refs, n_prev_k, n_prev_v, n_steps, aw, bw, hda, dkb):
    (x_ref, g_ref, w_ref, gm_ref, gq_ref, gk_ref, ca_ref, sa_ref, cb_ref, sb_ref) = refs[:10]
    n_prev = n_prev_k + n_prev_v
    prev_k, prev_v = refs[10:10 + n_prev_k], refs[10 + n_prev_k:10 + n_prev]
    (qa_ref, ka_ref, va_ref, vabf_ref, qb_ref, kb_ref, vb_ref, gb_ref,
     kbuf_ref, ksem_ref, vbuf_ref, vsem_ref, psem) = refs[10 + n_prev:]
    ts = x_ref.shape[0]
    rows = pl.ds(pl.multiple_of(pl.program_id(0) * ts, ts), ts)
    carried = ([pltpu.make_async_copy(src.at[rows], ka_ref.at[j, rows], psem) for j, src in enumerate(prev_k)]
               + [pltpu.make_async_copy(src.at[rows], va_ref.at[j, rows], psem) for j, src in enumerate(prev_v)])
    for cp in carried:
        cp.start()
    h = _rms(x_ref[...], g_ref[...]).astype(BF16)

    def proj(c0, width):
        return jnp.dot(h, w_ref[:, c0:c0 + width], preferred_element_type=F32)

    def qk_norm_rot(z, gain_ref, out_ref):
        ss = jnp.dot((z * z).astype(BF16), gm_ref[...], preferred_element_type=F32)
        y = z * lax.rsqrt(ss + EPS) * gain_ref[...]
        for c in range(aw // LANE):
            yc = y[:, c * LANE:(c + 1) * LANE]
            r = yc * ca_ref[...] + _rot_half(yc, hda // 2) * sa_ref[...]
            out_ref[:, c * LANE:(c + 1) * LANE] = r.astype(out_ref.dtype)

    def rot_b(z, out_ref, scale):
        for c in range(bw // LANE):
            zc = z[:, c * LANE:(c + 1) * LANE]
            r = zc * cb_ref[...] + _rot_half(zc, dkb // 2) * sb_ref[...]
            out_ref[:, c * LANE:(c + 1) * LANE] = (r * scale).astype(out_ref.dtype)

    qk_norm_rot(proj(0, aw), gq_ref, qa_ref)
    _emit_rows(kbuf_ref, ka_ref.at[n_prev_k], ksem_ref, n_steps,
               lambda slot: qk_norm_rot(proj(aw, aw), gk_ref, kbuf_ref.at[slot]))
    va = proj(2 * aw, aw)
    vabf_ref[...] = va.astype(BF16)

    def fill(slot):
        vbuf_ref[slot] = va
    _emit_rows(vbuf_ref, va_ref.at[n_prev_v], vsem_ref, n_steps, fill)
    rot_b(proj(3 * aw, bw), qb_ref, 1.0)
    rot_b(proj(3 * aw + bw, bw), kb_ref, dkb ** -0.5)
    vb_ref[...] = proj(3 * aw + 2 * bw, bw).astype(BF16)
    gb_ref[...] = proj(3 * aw + 3 * bw, bw).astype(BF16)
    for cp in carried:
        cp.wait()


def _project(x2d, seq, gain, w_bf, gm, gq_t, gk_t, tabs, prev_k, prev_v, *, aw, bw, hda, dkb):
    t, d = x2d.shape[0], gain.shape[1]
    ts = min(TOKEN_TILE, seq)
    n_s = seq // ts
    n_prev_k, n_prev_v = len(prev_k), len(prev_v)
    row = lambda i: (i, 0)
    const = lambda i: (0, 0)
    tab = lambda i: (i % n_s, 0)
    tab_spec = pl.BlockSpec((ts, LANE), tab)
    out_a = pl.BlockSpec((ts, aw), row)
    out_b = pl.BlockSpec((ts, bw), row)
    hbm = pl.BlockSpec(memory_space=pl.ANY)
    return pl.pallas_call(
        functools.partial(_proj_kernel, n_prev_k=n_prev_k, n_prev_v=n_prev_v, n_steps=t // ts,
                          aw=aw, bw=bw, hda=hda, dkb=dkb),
        grid=(t // ts,),
        in_specs=[pl.BlockSpec((ts, d), row), pl.BlockSpec((1, d), const),
                  pl.BlockSpec(w_bf.shape, const), pl.BlockSpec(gm.shape, const),
                  pl.BlockSpec((1, aw), const), pl.BlockSpec((1, aw), const),
                  tab_spec, tab_spec, tab_spec, tab_spec] + [hbm] * (n_prev_k + n_prev_v),
        out_specs=[out_a, hbm, hbm, out_a, out_b, out_b, out_b, out_b],
        out_shape=[jax.ShapeDtypeStruct((t, aw), BF16), jax.ShapeDtypeStruct((n_prev_k + 1, t, aw), F32),
                   jax.ShapeDtypeStruct((n_prev_v + 1, t, aw // LANE, LANE), F32),
                   jax.ShapeDtypeStruct((t, aw), BF16),
                   jax.ShapeDtypeStruct((t, bw), BF16), jax.ShapeDtypeStruct((t, bw), BF16),
                   jax.ShapeDtypeStruct((t, bw), BF16), jax.ShapeDtypeStruct((t, bw), BF16)],
        scratch_shapes=[pltpu.VMEM((2, ts, aw), F32), pltpu.SemaphoreType.DMA((2,)),
                        pltpu.VMEM((2, ts, aw), F32), pltpu.SemaphoreType.DMA((2,)), pltpu.SemaphoreType.DMA],
        compiler_params=_cparams("arbitrary"),
        name="proj",
    )(x2d, gain, w_bf, gm, gq_t, gk_t, *tabs, *prev_k, *prev_v)


def _scaled_queries(q, hda):
    return (q.astype(F32) * (hda ** -0.5 * LOG2E)).astype(BF16)


def _split_halves(k, hda):
    lo = (lax.broadcasted_iota(I32, k.shape, 1) & (LANE - 1)) < hda
    return jnp.where(lo, k, 0.0).astype(BF16), jnp.where(lo, 0.0, k).astype(BF16)


def _reset_softmax_state(m_ref, l_ref, acc_ref):
    m_ref[...] = jnp.full(m_ref.shape, NEG, F32)
    l_ref[...] = jnp.zeros(l_ref.shape, F32)
    acc_ref[...] = jnp.zeros(acc_ref.shape, F32)


def _softmax_blocks(qs, blocks, m_ref, l_ref, acc_ref):
    scores = [[lax.dot_general(kk, q, (((1,), (1,)), ((), ())), preferred_element_type=F32)
               for q, halves in zip(qs, k_halves) for kk in halves] for k_halves, _, _ in blocks]
    for (_, vts, valid), block_scores in zip(blocks, scores):
        probs, alphas = [], []
        for c, s in enumerate(block_scores):
            if valid is not None:
                s = jnp.where(valid, s, NEG)
            m_old = m_ref[c]
            m_new = jnp.maximum(m_old, jnp.max(s, axis=0, keepdims=True))
            alpha = jnp.exp2(m_old - m_new)
            p = jnp.exp2(s - m_new)
            l_ref[c] = alpha * l_ref[c] + jnp.sum(p, axis=0, keepdims=True)
            m_ref[c] = m_new
            probs.append(p.astype(BF16))
            alphas.append(alpha)
        for c, (p, alpha) in enumerate(zip(probs, alphas)):
            acc_ref[c] = alpha * acc_ref[c] + jnp.dot(vts[c // 2], p, preferred_element_type=F32)


def _diff_lambda(lv_ref, lam_init):
    lv = lv_ref[...]
    return (jnp.exp(jnp.sum(lv[0:1] * lv[1:2], axis=1, keepdims=True))
            - jnp.exp(jnp.sum(lv[2:3] * lv[3:4], axis=1, keepdims=True)) + lam_init)


def _finish_heads(lv_ref, sg_ref, o_ref, l_ref, acc_ref, lam_init):
    lam = _diff_lambda(lv_ref, lam_init)
    for h in range(o_ref.shape[2] // LANE):
        o_t = acc_ref[2 * h] / l_ref[2 * h] - lam * (acc_ref[2 * h + 1] / l_ref[2 * h + 1])
        o_ref[0, :, h * LANE:(h + 1) * LANE] = (_rms(o_t.T, sg_ref[...]) * (1.0 - lam_init)).astype(o_ref.dtype)


def _attn_prompt_kernel(q_ref, k_ref, v_ref, lv_ref, sg_ref, o_ref,
                        qs_ref, klo_ref, khi_ref, vt_ref, m_ref, l_ref, acc_ref, *, tile, hda, lam_init):
    qi = pl.program_id(1)
    n_pairs = q_ref.shape[2] // LANE
    n_blk = k_ref.shape[1] // tile

    @pl.when(qi == 0)
    def _stage_keys():
        def stage(j, carry):
            rows = pl.ds(pl.multiple_of(j * tile, tile), tile)
            klo_ref[rows, :], khi_ref[rows, :] = _split_halves(k_ref[0, rows, :], hda)
            v = v_ref[0, rows, :]
            for h in range(n_pairs):
                vt_ref[j * n_pairs + h] = v[:, h * LANE:(h + 1) * LANE].astype(F32).T.astype(BF16)
            return carry
        lax.fori_loop(0, n_blk, stage, 0)

    qs_ref[...] = _scaled_queries(q_ref[0], hda)
    _reset_softmax_state(m_ref, l_ref, acc_ref)

    cols = [slice(h * LANE, (h + 1) * LANE) for h in range(n_pairs)]

    def key_block(kb, valid):
        rows = pl.ds(pl.multiple_of(kb * tile, tile), tile)
        return ([(klo_ref[rows, c], khi_ref[rows, c]) for c in cols],
                [vt_ref[kb * n_pairs + h] for h in range(n_pairs)], valid)

    def attend(blocks):
        _softmax_blocks([qs_ref[:, c] for c in cols], blocks, m_ref, l_ref, acc_ref)

    def past_pair(j, carry):
        attend([key_block(2 * j, None), key_block(2 * j + 1, None)])
        return carry
    lax.fori_loop(0, qi // 2, past_pair, 0)
    shift = CHUNK.bit_length() - 1
    kchunk = lax.broadcasted_iota(I32, (tile, tile), 0) >> shift
    qchunk = lax.broadcasted_iota(I32, (tile, tile), 1) >> shift
    own = kchunk <= qchunk

    @pl.when(qi % 2 == 1)
    def _odd():
        attend([key_block(qi - 1, None), key_block(qi, own)])

    @pl.when(qi % 2 == 0)
    def _even():
        attend([key_block(qi, own)])
    _finish_heads(lv_ref, sg_ref, o_ref, l_ref, acc_ref, lam_init)


def _attn_cached_kernel(q_ref, kct_ref, vc_ref, kn_ref, vn_ref, lv_ref, sg_ref, o_ref,
                        qlo_ref, qhi_ref, m_ref, l_ref, acc_ref, *, hda, lam_init):
    pi = pl.program_id(1)
    n_pairs = q_ref.shape[2] // LANE
    cols = [slice(h * LANE, (h + 1) * LANE) for h in range(n_pairs)]

    @pl.when(pi == 0)
    def _start():
        qlo_ref[...], qhi_ref[...] = _split_halves(_scaled_queries(q_ref[0], hda), hda)
        _reset_softmax_state(m_ref, l_ref, acc_ref)

    def update(c, s, v):
        m_old = m_ref[c]
        m_new = jnp.maximum(m_old, jnp.max(s, axis=1, keepdims=True))
        alpha = jnp.exp2(m_old - m_new)
        p = jnp.exp2(s - m_new)
        l_ref[c] = alpha * l_ref[c] + jnp.sum(p, axis=1, keepdims=True)
        acc_ref[c] = alpha * acc_ref[c] + jnp.dot(p.astype(BF16), v, preferred_element_type=F32)
        m_ref[c] = m_new

    def attend(score_fn, value_fn):
        scores = [score_fn(q[:, c], h) for h, c in enumerate(cols) for q in (qlo_ref, qhi_ref)]
        for h in range(n_pairs):
            v = value_fn(h)
            update(2 * h, scores[2 * h], v)
            update(2 * h + 1, scores[2 * h + 1], v)

    attend(lambda q, h: jnp.dot(q, kct_ref[0, cols[h], :].astype(BF16), preferred_element_type=F32),
           lambda h: vc_ref[0, :, h, :].astype(BF16))

    @pl.when(pi == pl.num_programs(1) - 1)
    def _finish():
        attend(lambda q, h: lax.dot_general(q, kn_ref[0, :, cols[h]].astype(BF16), (((1,), (1,)), ((), ())),
                                            preferred_element_type=F32),
               lambda h: vn_ref[0, :, cols[h]])
        lam = _diff_lambda(lv_ref, lam_init)
        for h, c in enumerate(cols):
            o = acc_ref[2 * h] / l_ref[2 * h] - lam * (acc_ref[2 * h + 1] / l_ref[2 * h + 1])
            o_ref[0, :, c] = (_rms(o, sg_ref[...]) * (1.0 - lam_init)).astype(o_ref.dtype)


def _diff_attention(qa, k_all, k_layer, va, cache, lam_vec, subln, *, causal, hda, lam_init):
    b, s, w = qa.shape
    n_chains = 2 * (w // LANE)
    const = lambda bi, si: (0, 0)
    whole = pl.BlockSpec((1, s, w), lambda bi, si: (bi, 0, 0))
    whole_k = pl.BlockSpec((None, 1, s, w), lambda bi, si: (k_layer, bi, 0, 0))
    small = [pl.BlockSpec(lam_vec.shape, const), pl.BlockSpec((1, LANE), const)]

    def state(tq):
        return [pltpu.VMEM((tq, w), BF16), pltpu.VMEM((n_chains, 1, tq), F32),
                pltpu.VMEM((n_chains, 1, tq), F32), pltpu.VMEM((n_chains, LANE, tq), F32)]

    if causal:
        tile = min(ATTN_TILE, s)
        q_blk = pl.BlockSpec((1, tile, w), lambda bi, qi: (bi, qi, 0))
        qs, *softmax_state = state(tile)
        return pl.pallas_call(
            functools.partial(_attn_prompt_kernel, tile=tile, hda=hda, lam_init=lam_init),
            grid=(b, s // tile),
            in_specs=[q_blk, whole_k, whole] + small,
            out_specs=q_blk,
            out_shape=jax.ShapeDtypeStruct((b, s, w), BF16),
            scratch_shapes=[qs, pltpu.VMEM((s, w), BF16), pltpu.VMEM((s, w), BF16),
                            pltpu.VMEM((s // tile * (w // LANE), LANE, tile), BF16)] + softmax_state,
            compiler_params=_cparams("parallel", "arbitrary"),
            name="diff_attn",
        )(qa, k_all, va, lam_vec, subln)
    ckt, cv, li = cache
    past = ckt.shape[3]
    tkc = min(CACHE_TILE, past)
    return pl.pallas_call(
        functools.partial(_attn_cached_kernel, hda=hda, lam_init=lam_init),
        grid=(b, past // tkc),
        in_specs=[whole, pl.BlockSpec((None, 1, w, tkc), lambda bi, pi: (li, bi, 0, pi)),
                  pl.BlockSpec((None, 1, tkc, *cv.shape[3:]), lambda bi, pi: (li, bi, pi, 0, 0)),
                  whole_k, whole] + small,
        out_specs=whole,
        out_shape=jax.ShapeDtypeStruct((b, s, w), BF16),
        scratch_shapes=[pltpu.VMEM((s, w), BF16), pltpu.VMEM((s, w), BF16), pltpu.VMEM((n_chains, s, 1), F32),
                        pltpu.VMEM((n_chains, s, 1), F32), pltpu.VMEM((n_chains, s, LANE), F32)],
        compiler_params=_cparams("parallel", "arbitrary"),
        name="diff_attn_cached",
    )(qa, ckt, cv, k_all, va, lam_vec, subln)


def _retention_kernel(*refs, has_state, chunk):
    if has_state:
        q_ref, k_ref, v_ref, g_ref, rg_ref, s0_ref, y_ref, so_ref, st_ref, decay_ref = refs
    else:
        q_ref, k_ref, v_ref, g_ref, rg_ref, y_ref, so_ref, st_ref, decay_ref = refs
    ci = pl.program_id(1)
    n_heads = q_ref.shape[2] // LANE
    log_gammas = [float(np.log(1.0 - 2.0 ** (-5.0 - h))) for h in range(n_heads)]

    @pl.when(ci == 0)
    def _init():
        st_ref[...] = s0_ref[0] if has_state else jnp.zeros(st_ref.shape, F32)
        row = lax.broadcasted_iota(I32, (chunk, chunk), 0)
        col = lax.broadcasted_iota(I32, (chunk, chunk), 1)
        dist = (row - col).astype(F32)
        for h, lg in enumerate(log_gammas):
            decay_ref[h] = jnp.where(dist >= 0, jnp.exp(lg * jnp.maximum(dist, 0.0)), 0.0)

    idx = lax.broadcasted_iota(I32, (chunk, 1), 0).astype(F32)
    heads = list(enumerate(log_gammas))
    cols = [slice(h * LANE, (h + 1) * LANE) for h in range(n_heads)]
    qk = [lax.dot_general(q_ref[0, :, c], k_ref[0, :, c], (((1,), (1,)), ((), ())), preferred_element_type=F32)
          for c in cols]
    cross = [jnp.dot(q_ref[0, :, c], st_ref[h].astype(BF16), preferred_element_type=F32)
             for h, c in enumerate(cols)]
    kv = []
    for (h, lg), c in zip(heads, cols):
        kz = (k_ref[0, :, c].astype(F32) * jnp.exp(lg * (chunk - 1.0 - idx))).astype(BF16)
        kv.append(lax.dot_general(kz, v_ref[0, :, c], (((0,), (0,)), ((), ())), preferred_element_type=F32))
    intra = [jnp.dot((qk[h] * decay_ref[h]).astype(BF16), v_ref[0, :, c], preferred_element_type=F32)
             for h, c in enumerate(cols)]
    for (h, lg), c in zip(heads, cols):
        out = intra[h] + cross[h] * jnp.exp(lg * (idx + 1.0))
        st_ref[h] = float(np.exp(lg * chunk)) * st_ref[h] + kv[h]
        gate = g_ref[0, :, c].astype(F32)
        y_ref[0, :, c] = (_rms(out, rg_ref[...]) * (gate * jax.nn.sigmoid(gate))).astype(y_ref.dtype)

    @pl.when(ci == pl.num_programs(1) - 1)
    def _emit_state():
        so_ref[0] = st_ref[...]


def _retention(qb, kb, vb, gb, ret_g, state0):
    b, s, w = qb.shape
    n_heads = w // LANE
    chunk = min(RET_CHUNK, s)
    has_state = state0 is not None
    blk = pl.BlockSpec((1, chunk, w), lambda bi, ci: (bi, ci, 0))
    st_spec = pl.BlockSpec((1, n_heads, LANE, LANE), lambda bi, ci: (bi, 0, 0, 0))
    in_specs = [blk, blk, blk, blk, pl.BlockSpec((1, LANE), lambda bi, ci: (0, 0))]
    args = [qb, kb, vb, gb, ret_g]
    if has_state:
        in_specs.append(st_spec)
        args.append(state0)
    return pl.pallas_call(
        functools.partial(_retention_kernel, has_state=has_state, chunk=chunk),
        grid=(b, s // chunk),
        in_specs=in_specs,
        out_specs=[blk, st_spec],
        out_shape=[jax.ShapeDtypeStruct((b, s, w), BF16), jax.ShapeDtypeStruct((b, n_heads, LANE, LANE), F32)],
        scratch_shapes=[pltpu.VMEM((n_heads, LANE, LANE), F32), pltpu.VMEM((n_heads, chunk, chunk), F32)],
        compiler_params=_cparams("parallel", "arbitrary"),
        name="retention",
    )(*args)


def _router_logits(h_bf, wr_ref, br_ref):
    return jnp.dot(h_bf, wr_ref[...], preferred_element_type=F32) + br_ref[...]


def _first_row_of_max(vals, row):
    vmax = jnp.max(vals, axis=0, keepdims=True)
    first = jnp.min(jnp.where(vals == vmax, row.astype(F32), float(ROUTER_LANES)), axis=0, keepdims=True)
    return first.astype(I32)


def _mix_kernel(x_ref, ya_ref, yb_ref, woa_ref, wob_ref, g_ref, wrt_ref, brt_ref,
                x1_ref, cls_ref, cnt_ref, buf_ref, sem_ref, *, n_steps, n_groups, n_experts):
    x1 = (x_ref[...] + jnp.dot(ya_ref[...], woa_ref[...], preferred_element_type=F32)
          + jnp.dot(yb_ref[...], wob_ref[...], preferred_element_type=F32))

    def fill(slot):
        buf_ref[slot] = x1
    _emit_rows(buf_ref, x1_ref, sem_ref, n_steps, fill)
    h = _rms(x1, g_ref[...]).astype(BF16)
    logits = lax.dot_general(wrt_ref[...], h, (((1,), (1,)), ((), ())), preferred_element_type=F32) + brt_ref[...]
    row = lax.broadcasted_iota(I32, logits.shape, 0)
    g_sel = _first_row_of_max(jnp.where(row < n_groups, logits, NEG), row)
    e_lo = n_groups + g_sel * n_experts
    e_logits = jnp.where((row >= e_lo) & (row < e_lo + n_experts), logits, NEG)
    top1 = _first_row_of_max(e_logits, row)
    top2 = _first_row_of_max(jnp.where(row == top1, NEG, e_logits), row)
    ea = jnp.minimum(top1, top2) - e_lo
    eb = jnp.maximum(top1, top2) - e_lo
    pair = ((ea * (2 * n_experts - 1 - ea)) >> 1) + (eb - ea - 1)
    n_pairs = n_experts * (n_experts - 1) // 2
    cls = g_sel * n_pairs + pair
    cls_ref[0] = cls

    @pl.when(pl.program_id(0) == 0)
    def _zero():
        cnt_ref[...] = jnp.zeros(cnt_ref.shape, F32)

    cnt_ref[...] += jnp.sum(jnp.where(row == cls, 1.0, 0.0), axis=1, keepdims=True)


def _mix(x2d, ya, yb, wo_a, wo_b, gain, wrt, brt, *, n_groups, n_experts):
    t, d = x2d.shape[0], gain.shape[1]
    ts = min(TOKEN_TILE, t)
    row = lambda i: (i, 0)
    const = lambda i: (0, 0)
    x1_shape = _row_tiled(t, d)
    return pl.pallas_call(
        functools.partial(_mix_kernel, n_steps=t // ts, n_groups=n_groups, n_experts=n_experts),
        grid=(t // ts,),
        in_specs=[pl.BlockSpec((ts, d), row), pl.BlockSpec((ts, ya.shape[1]), row),
                  pl.BlockSpec((ts, yb.shape[1]), row), pl.BlockSpec(wo_a.shape, const),
                  pl.BlockSpec(wo_b.shape, const), pl.BlockSpec((1, d), const),
                  pl.BlockSpec(wrt.shape, const), pl.BlockSpec(brt.shape, const)],
        out_specs=[pl.BlockSpec(memory_space=pl.ANY), pl.BlockSpec((1, 1, ts), lambda i: (i, 0, 0)),
                   pl.BlockSpec((ROUTER_LANES, 1), const)],
        out_shape=[jax.ShapeDtypeStruct(x1_shape, F32), jax.ShapeDtypeStruct((t // ts, 1, ts), I32),
                   jax.ShapeDtypeStruct((ROUTER_LANES, 1), F32)],
        scratch_shapes=[pltpu.VMEM((2, ts, d), F32), pltpu.SemaphoreType.DMA((2,))],
        compiler_params=_cparams("arbitrary"),
        name="mix",
    )(x2d, ya, yb, wo_a, wo_b, gain, wrt, brt)


ISSUE_UNROLL = 8
DMA_LANES = 2


def _row_copy(src, src_row, dst, dst_row, sem):
    return pltpu.make_async_copy(src.at[src_row], dst.at[dst_row], sem)


def _rows_copy(src, dst, n, sem):
    return pltpu.make_async_copy(src.at[pl.ds(0, n)], dst.at[pl.ds(0, n)], sem)


def _slot_kernel(cls_ref, start_ref, pos_ref, next_ref, earlier_ref):
    ts = cls_ref.shape[2]

    @pl.when(pl.program_id(0) == 0)
    def _init():
        next_ref[...] = start_ref[...]
        before = lax.broadcasted_iota(I32, (ts, ts), 0) < lax.broadcasted_iota(I32, (ts, ts), 1)
        earlier_ref[...] = jnp.where(before, 1.0, 0.0).astype(BF16)

    member = lax.broadcasted_iota(I32, (ROUTER_LANES, ts), 0) == cls_ref[0]
    onehot = jnp.where(member, 1.0, 0.0)
    ahead = jnp.dot(onehot.astype(BF16), earlier_ref[...], preferred_element_type=F32)
    pos_ref[0] = jnp.sum(onehot * (ahead + next_ref[...]), axis=0, keepdims=True).astype(I32)
    next_ref[...] += jnp.sum(onehot, axis=1, keepdims=True)


def _slots(cls, slot_start):
    n_tiles, _, ts = cls.shape
    blk = pl.BlockSpec((1, 1, ts), lambda i: (i, 0, 0))
    return pl.pallas_call(
        _slot_kernel,
        grid=(n_tiles,),
        in_specs=[blk, pl.BlockSpec((ROUTER_LANES, 1), lambda i: (0, 0))],
        out_specs=blk,
        out_shape=jax.ShapeDtypeStruct(cls.shape, I32),
        scratch_shapes=[pltpu.VMEM((ROUTER_LANES, 1), F32), pltpu.VMEM((ts, ts), BF16)],
        compiler_params=_cparams("arbitrary"),
        name="slots",
    )(cls, slot_start.astype(F32).reshape(ROUTER_LANES, 1))


def _dispatch_kernel(fill_ref, end_ref, pos_ref, x_ref, xs_ref, zero_ref, sem, *, n_cls):
    step = pl.program_id(0)
    ts = x_ref.shape[0]

    @pl.when(step == 0)
    def _init():
        zero_ref[...] = jnp.zeros(zero_ref.shape, F32)

    def place(j, carry):
        for lane in range(DMA_LANES):
            r = j * DMA_LANES + lane
            _row_copy(x_ref, r, xs_ref, pos_ref[0, 0, r], sem).start(priority=lane)
        return carry
    lax.fori_loop(0, ts // DMA_LANES, place, 0, unroll=ISSUE_UNROLL // DMA_LANES)
    _rows_copy(x_ref, xs_ref, ts, sem).wait()

    @pl.when(step == pl.num_programs(0) - 1)
    def _pad():
        tile = zero_ref.shape[0]

        def pad_copies(c, act):
            row, n = fill_ref[c], end_ref[c] - fill_ref[c]
            p = tile // 2
            while p:
                piece = pltpu.make_async_copy(zero_ref.at[pl.ds(0, p)], xs_ref.at[pl.ds(row, p)], sem)
                pl.when((n & p) != 0)(functools.partial(act, piece))
                row = row + (n & p)
                p //= 2

        def start_class(c, carry):
            pad_copies(c, lambda piece: piece.start())
            return carry
        lax.fori_loop(0, n_cls, start_class, 0)

        def wait_class(c, carry):
            pad_copies(c, lambda piece: piece.wait())
            return carry
        lax.fori_loop(0, n_cls, wait_class, 0)

        first_free = lax.div(end_ref[n_cls - 1], tile)
        n_tiles = xs_ref.shape[0] // tile

        def tile_copy(ti):
            return pltpu.make_async_copy(zero_ref, xs_ref.at[pl.ds(pl.multiple_of(ti * tile, tile), tile)], sem)

        def clear(ti, carry):
            tile_copy(ti).start()
            return carry
        lax.fori_loop(first_free, n_tiles, clear, 0)

        def clear_wait(ti, carry):
            tile_copy(ti).wait()
            return carry
        lax.fori_loop(first_free, n_tiles, clear_wait, 0)


def _dispatch(x1, pos, fill_from, slot_end, n_slots, *, n_cls, tile):
    n_tiles, _, ts = pos.shape
    return pl.pallas_call(
        functools.partial(_dispatch_kernel, n_cls=n_cls),
        grid_spec=pltpu.PrefetchScalarGridSpec(
            num_scalar_prefetch=2,
            grid=(n_tiles,),
            in_specs=[pl.BlockSpec((1, 1, ts), lambda i, *_: (i, 0, 0), memory_space=pltpu.SMEM),
                      _rows_spec(ts, x1.shape)],
            out_specs=pl.BlockSpec(memory_space=pl.ANY),
            scratch_shapes=[pltpu.VMEM((tile, *x1.shape[1:]), F32), pltpu.SemaphoreType.DMA],
        ),
        out_shape=jax.ShapeDtypeStruct((n_slots, *x1.shape[1:]), F32),
        compiler_params=_cparams("arbitrary"),
        name="dispatch",
    )(fill_from, slot_end, pos, x1)


def _gather_rows(pos_ref, ys_ref, rows_ref, sem):
    ts = rows_ref.shape[0]

    def fetch(j, carry):
        for lane in range(DMA_LANES):
            r = j * DMA_LANES + lane
            _row_copy(ys_ref, pos_ref[0, 0, r], rows_ref, r, sem).start(priority=lane)
        return carry
    lax.fori_loop(0, ts // DMA_LANES, fetch, 0, unroll=ISSUE_UNROLL // DMA_LANES)
    _rows_copy(ys_ref, rows_ref, ts, sem).wait()


def _unpermute_kernel(pos_ref, ys_ref, o_ref, stage_ref, sem, out_sem, *, n_steps):
    def fill(slot):
        _gather_rows(pos_ref, ys_ref, stage_ref.at[slot], sem)
    _emit_rows(stage_ref, o_ref, out_sem, n_steps, fill)


def _unpermute(ys, pos, t):
    n_tiles, _, ts = pos.shape
    hbm = pl.BlockSpec(memory_space=pl.ANY)
    return pl.pallas_call(
        functools.partial(_unpermute_kernel, n_steps=n_tiles),
        grid=(n_tiles,),
        in_specs=[pl.BlockSpec((1, 1, ts), lambda i: (i, 0, 0), memory_space=pltpu.SMEM), hbm],
        out_specs=hbm,
        out_shape=jax.ShapeDtypeStruct((t, ys.shape[1] * ys.shape[2]), F32),
        scratch_shapes=[pltpu.VMEM((2, ts, *ys.shape[1:]), F32), pltpu.SemaphoreType.DMA,
                        pltpu.SemaphoreType.DMA((2,))],
        compiler_params=_cparams("arbitrary"),
        name="unpermute",
    )(pos, ys)


def _expert_kernel(tg_ref, ta_ref, tb_ref, nv_ref, xs_ref, g_ref, wr_ref, br_ref,
                   wga_ref, wua_ref, wda_ref, wgb_ref, wub_ref, wdb_ref, ys_ref,
                   xbuf_ref, in_sem_ref, buf_ref, sem_ref, *, n_steps, n_groups, n_experts):
    i = pl.program_id(0)
    in_slot = _fetch_rows(xs_ref, xbuf_ref, in_sem_ref, n_steps)

    def run():
        x = xbuf_ref[in_slot]
        h = _rms(x, g_ref[...]).astype(BF16)
        logits = _router_logits(h, wr_ref, br_ref)
        gates = [jnp.dot(h, w[0, 0], preferred_element_type=F32) for w in (wga_ref, wgb_ref)]
        ups = [jnp.dot(h, w[0, 0], preferred_element_type=F32) for w in (wua_ref, wub_ref)]
        lane = lax.broadcasted_iota(I32, logits.shape, 1)
        grp, ea, eb = tg_ref[i], ta_ref[i], tb_ref[i]

        def pick(idx):
            return jnp.sum(jnp.where(lane == idx, logits, 0.0), axis=1, keepdims=True)

        g_logits = jnp.where(lane < n_groups, logits, NEG)
        g_max = jnp.max(g_logits, axis=1, keepdims=True)
        g_w = jnp.exp(pick(grp) - g_max) / jnp.sum(jnp.exp(g_logits - g_max), axis=1, keepdims=True)
        e_lo = n_groups + grp * n_experts
        w_a = jax.nn.sigmoid(pick(e_lo + ea) - pick(e_lo + eb))
        acts = [(gate * jax.nn.sigmoid(gate) * up * comb).astype(BF16)
                for gate, up, comb in zip(gates, ups, (g_w * w_a, g_w * (1.0 - w_a)))]
        return (x + jnp.dot(acts[0], wda_ref[0, 0], preferred_element_type=F32)
                + jnp.dot(acts[1], wdb_ref[0, 0], preferred_element_type=F32))

    def fill(slot):
        @pl.when(nv_ref[i] == 0)
        def _unused_tile():
            buf_ref[slot] = jnp.zeros(buf_ref.shape[1:], F32)

        @pl.when(nv_ref[i] > 0)
        def _used_tile():
            buf_ref[slot] = run()
    _emit_rows(buf_ref, ys_ref, sem_ref, n_steps, fill)


def _experts(xs, gain, wr, br, wg, wu, wd, meta, *, tile, n_groups, n_experts):
    n_slots, d = xs.shape[0], gain.shape[1]
    de = wg.shape[-1]
    n_steps = n_slots // tile
    tile_g, tile_a, tile_b, tile_nv = meta
    const = lambda i, *_: (0, 0)
    sel_a = lambda i, tg, ta, tb, nv: (tg[i], ta[i], 0, 0)
    sel_b = lambda i, tg, ta, tb, nv: (tg[i], tb[i], 0, 0)
    up_spec_a = pl.BlockSpec((1, 1, d, de), sel_a)
    up_spec_b = pl.BlockSpec((1, 1, d, de), sel_b)
    hbm = pl.BlockSpec(memory_space=pl.ANY)
    return pl.pallas_call(
        functools.partial(_expert_kernel, n_steps=n_steps, n_groups=n_groups, n_experts=n_experts),
        grid_spec=pltpu.PrefetchScalarGridSpec(
            num_scalar_prefetch=4,
            grid=(n_steps,),
            in_specs=[hbm, pl.BlockSpec((1, d), const),
                      pl.BlockSpec(wr.shape, const), pl.BlockSpec(br.shape, const),
                      up_spec_a, up_spec_a, pl.BlockSpec((1, 1, de, d), sel_a),
                      up_spec_b, up_spec_b, pl.BlockSpec((1, 1, de, d), sel_b)],
            out_specs=hbm,
            scratch_shapes=[pltpu.VMEM((2, tile, d), F32), pltpu.SemaphoreType.DMA((2,)),
                            pltpu.VMEM((2, tile, d), F32), pltpu.SemaphoreType.DMA((2,))],
        ),
        out_shape=jax.ShapeDtypeStruct(xs.shape, F32),
        compiler_params=_cparams("arbitrary"),
        name="experts",
    )(tile_g, tile_a, tile_b, tile_nv, xs, gain, wr, br, wg, wu, wd, wg, wu, wd)


def _class_layout(counts, n_cls, n_pairs, n_experts, n_tiles, tile):
    counts = counts[:n_cls, 0].astype(I32)
    tiles_c = (counts + tile - 1) // tile
    tile_end = jnp.cumsum(tiles_c)
    tile_start = tile_end - tiles_c
    n_active = tile_end[-1]
    pad = ROUTER_LANES - n_cls
    slot_start = jnp.pad(tile_start * tile, (0, pad))
    slot_end = jnp.pad(tile_end * tile, (0, pad))
    fill_from = jnp.pad(tile_start * tile + counts, (0, pad))
    t_idx = jnp.arange(n_tiles, dtype=I32)
    t_cls = jnp.minimum(jnp.sum((tile_end[None, :] <= t_idx[:, None]).astype(I32), axis=1), n_cls - 1)
    active = t_idx < n_active
    t_nv = jnp.where(active, jnp.clip(counts[t_cls] - (t_idx - tile_start[t_cls]) * tile, 0, tile), 0)
    t_cls = jnp.where(active, t_cls, t_cls[jnp.maximum(n_active - 1, 0)])
    pair_a, pair_b = np.triu_indices(n_experts, 1)
    pair = t_cls % n_pairs
    meta = (t_cls // n_pairs, jnp.asarray(pair_a, I32)[pair], jnp.asarray(pair_b, I32)[pair], t_nv.astype(I32))
    return slot_start.astype(I32), fill_from.astype(I32), slot_end.astype(I32), meta


def _rope_tables(pos, group, signed_half):
    half = group // 2
    lane = np.arange(LANE)
    inv = (ROPE_THETA ** (-jnp.arange(0, group, 2, dtype=F32) / group))[lane % half]
    ang = pos.astype(F32)[:, None] * inv[None, :]
    sign = np.where((lane % group) < signed_half, -1.0, 1.0).astype(np.float32)
    return jnp.cos(ang), jnp.sin(ang) * sign[None, :]


def _layer(x2d, b, s, pos, cache, state0, p, li, dims, prev_v):
    aw, bw, hda, dkb = dims["aw"], dims["bw"], dims["hda"], dims["dkb"]
    n_groups, n_experts = dims["n_groups"], dims["n_experts"]
    lam_init = 0.8 - 0.6 * float(np.exp(-0.3 * li))
    t = b * s
    tabs = (*_rope_tables(pos, hda, hda // 2), *_rope_tables(pos, dkb, dkb // 2))
    qa, ka, va, va_bf, qb, kb, vb, gb = _project(x2d, s, p["norm_attn"], p["w_in"], p["gm"], p["gq"], p["gk"], tabs,
                                                 [], prev_v, aw=aw, bw=bw, hda=hda, dkb=dkb)
    shape3 = lambda a: a.reshape(b, s, a.shape[-1])
    ya = _diff_attention(shape3(qa), ka.reshape(-1, b, s, aw), 0, shape3(va_bf), cache,
                         p["lam_vec"], p["subln"], causal=cache is None, hda=hda, lam_init=lam_init)
    yb, new_state = _retention(shape3(qb), shape3(kb), shape3(vb), shape3(gb), p["ret_norm"], state0)
    x1, cls, counts = _mix(x2d, ya.reshape(t, aw), yb.reshape(t, bw), p["wo_a"], p["wo_b"], p["norm_ffn"],
                           p["wrt"], p["brt"], n_groups=n_groups, n_experts=n_experts)
    n_pairs = n_experts * (n_experts - 1) // 2
    n_cls = n_groups * n_pairs
    tile = EXPERT_TILE if t >= EXPERT_TILE * n_cls else SMALL_EXPERT_TILE
    n_tiles = t // tile + n_cls
    slot_start, fill_from, slot_end, meta = _class_layout(counts, n_cls, n_pairs, n_experts, n_tiles, tile)
    slot_of = _slots(cls, slot_start)
    xs = _dispatch(x1, slot_of, fill_from, slot_end, n_tiles * tile, n_cls=n_cls, tile=tile)
    ys = _experts(xs, p["norm_ffn"], p["wr"], p["br"], p["w_gate"], p["w_up"], p["w_down"], meta,
                  tile=tile, n_groups=n_groups, n_experts=n_experts)
    return _unpermute(ys, slot_of, t), ka, va, new_state


def kernel(x_prompt, x_sample, cache_k, cache_v, state_ret, norm_attn, w_in, q_norm, k_norm, lam_vec,
           subln, ret_norm, w_out, norm_ffn, w_group, b_group, w_expert, b_expert, w_gate, w_up, w_down):
    depth, dec_b, past, heads2, hda = cache_k.shape
    _, _, n_ret, dkb, dvb = state_ret.shape
    n_groups, n_experts = w_gate.shape[1], w_gate.shape[2]
    aw, bw = heads2 * hda, n_ret * dkb
    assert 2 * hda == LANE and dkb == LANE and dvb == LANE and aw % LANE == 0
    assert n_groups * (1 + n_experts) <= ROUTER_LANES
    assert n_groups * n_experts * (n_experts - 1) // 2 <= ROUTER_LANES
    dims = dict(aw=aw, bw=bw, hda=hda, dkb=dkb, n_groups=n_groups, n_experts=n_experts)
    d = x_prompt.shape[-1]
    group_of = np.arange(aw) // hda
    gm = jnp.asarray((group_of[:, None] == group_of[None, :]) / hda, BF16)
    row = lambda v: v.reshape(1, -1).astype(F32)

    def layer_params(li):
        wr = jnp.concatenate([w_group[li], w_expert[li].reshape(d, n_groups * n_experts)], axis=1)
        br = jnp.concatenate([b_group[li], b_expert[li].reshape(-1)])
        lane_pad = ROUTER_LANES - wr.shape[1]
        return dict(
            norm_attn=row(norm_attn[li]), w_in=w_in[li].astype(BF16), gm=gm,
            gq=row(jnp.tile(q_norm[li], heads2)), gk=row(jnp.tile(k_norm[li], heads2)),
            lam_vec=lam_vec[li].astype(F32), subln=row(subln[li]), ret_norm=row(ret_norm[li]),
            wo_a=w_out[li, :aw].astype(BF16), wo_b=w_out[li, aw:].astype(BF16), norm_ffn=row(norm_ffn[li]),
            wr=jnp.pad(wr, ((0, 0), (0, lane_pad))).astype(BF16), br=row(jnp.pad(br, (0, lane_pad))),
            wrt=jnp.pad(wr, ((0, 0), (0, lane_pad))).T.astype(BF16),
            brt=jnp.pad(br, (0, lane_pad)).reshape(-1, 1).astype(F32),
            w_gate=w_gate[li].astype(BF16), w_up=w_up[li].astype(BF16), w_down=w_down[li].astype(BF16))

    cache_kt = jnp.transpose(cache_k, (0, 1, 3, 4, 2)).reshape(depth, dec_b, aw, past)
    pos_p = jnp.arange(x_prompt.shape[1])
    pos_s = past + jnp.arange(x_sample.shape[1])
    (bp, sp), (bs, ss) = x_prompt.shape[:2], x_sample.shape[:2]
    xp, xs = x_prompt.reshape(bp * sp, d), x_sample.reshape(bs * ss, d)
    kv_p, kv_s, st_p, st_s = [], [], [], []
    for li in range(depth):
        p = layer_params(li)
        prev = lambda kv: [v[0] for _, v in kv] if li == depth - 1 else []
        xp, ka, va, st = _layer(xp, bp, sp, pos_p, None, None, p, li, dims, prev(kv_p))
        kv_p.append((ka.reshape(bp, sp, heads2, hda), va))
        st_p.append(st)
        xs, ka, va, st = _layer(xs, bs, ss, pos_s, (cache_kt, cache_v, li), state_ret[li], p, li, dims, prev(kv_s))
        kv_s.append((ka.reshape(bs, ss, heads2, hda), va))
        st_s.append(st)
    return (xp.reshape(bp, sp, d), xs.reshape(bs, ss, d),
            jnp.stack([k for k, _ in kv_p]), kv_p[-1][1].reshape(depth, bp, sp, heads2 // 2, 2 * hda),
            jnp.stack(st_p),
            jnp.stack([k for k, _ in kv_s]), kv_s[-1][1].reshape(depth, bs, ss, heads2 // 2, 2 * hda),
            jnp.stack(st_s))
```

```python
import functools

import jax
import jax.numpy as jnp
import numpy as np
from jax import lax
from jax.experimental import pallas as pl
from jax.experimental.pallas import tpu as pltpu

F32 = jnp.float32
BF16 = jnp.bfloat16
I32 = jnp.int32

CHUNK = 64
ROPE_THETA = 10000.0
EPS = 1e-6
NEG = -1e30
LANE = 128
SUBLANES = 8
ROUTER_LANES = 128
VMEM_LIMIT_BYTES = 52 * 1024 * 1024

TOKEN_TILE = 512
ATTN_TILE = 256
CACHE_TILE = 512
LOG2E = 1.4426950408889634
RET_CHUNK = 256
EXPERT_TILE = 256
SMALL_EXPERT_TILE = 16


def _cparams(*sem):
    return pltpu.CompilerParams(dimension_semantics=sem, vmem_limit_bytes=VMEM_LIMIT_BYTES)


def _rms(x, gain):
    ms = jnp.mean(x * x, axis=-1, keepdims=True)
    return x * lax.rsqrt(ms + EPS) * gain


def _rot_half(y, half):
    if 2 * half == LANE:
        return pltpu.roll(y, half, axis=1)
    lane = lax.broadcasted_iota(I32, y.shape, 1)
    first = (lane & (2 * half - 1)) < half
    return jnp.where(first, pltpu.roll(y, LANE - half, axis=1), pltpu.roll(y, half, axis=1))


def _row_tiled(t, d):
    assert d % (SUBLANES * LANE) == 0
    return (t, d // LANE, LANE)


def _rows_spec(ts, shape):
    zeros = (0,) * (len(shape) - 1)
    return pl.BlockSpec((ts, *shape[1:]), lambda i, *s: (i, *zeros))


def _retile_copies(buf_ref, slot, out_ref, row0, sem_ref):
    rows = buf_ref.shape[1]
    copies = []
    for c in range(out_ref.shape[1] if len(out_ref.shape) == 3 else buf_ref.shape[2]):
        lanes = pl.ds(c * LANE, LANE)
        if len(out_ref.shape) == 3:
            src, dst = buf_ref.at[slot, :, lanes], out_ref.at[pl.ds(row0, rows), c, :]
        else:
            src, dst = buf_ref.at[slot, :, c, :], out_ref.at[pl.ds(row0, rows), lanes]
        copies.append(pltpu.make_async_copy(src, dst, sem_ref.at[slot]))
    return copies


def _fetch_rows(src_ref, buf_ref, sem_ref, n_steps):
    step = pl.program_id(0)
    slot = lax.rem(step, 2)
    rows = buf_ref.shape[1]

    def copies(s, row0):
        return [pltpu.make_async_copy(src_ref.at[pl.ds(row0, rows), c, :],
                                      buf_ref.at[s, :, pl.ds(c * LANE, LANE)], sem_ref.at[s])
                for c in range(src_ref.shape[1])]

    @pl.when(step == 0)
    def _first():
        for cp in copies(0, 0):
            cp.start()

    @pl.when(step + 1 < n_steps)
    def _next():
        for cp in copies(1 - slot, pl.multiple_of((step + 1) * rows, rows)):
            cp.start()

    for cp in copies(slot, 0):
        cp.wait()
    return slot


def _emit_rows(buf_ref, out_ref, sem_ref, n_steps, fill):
    step = pl.program_id(0)
    slot = lax.rem(step, 2)
    rows = buf_ref.shape[1]

    def drain(s):
        for cp in _retile_copies(buf_ref, s, out_ref, 0, sem_ref):
            cp.wait()

    @pl.when(step >= 2)
    def _reuse():
        drain(slot)

    fill(slot)
    for cp in _retile_copies(buf_ref, slot, out_ref, pl.multiple_of(step * rows, rows), sem_ref):
        cp.start()

    @pl.when(step == n_steps - 1)
    def _finish():
        drain(slot)
        if n_steps >= 2:
            drain(1 - slot)


def _proj_kernel(x_ref, g_ref, w_ref, gm_ref, gq_ref, gk_ref, ca_ref, sa_ref, cb_ref, sb_ref,
                 qa_ref, ka_ref, va_ref, vabf_ref, qb_ref, kb_ref, vb_ref, gb_ref, vbuf_ref, vsem_ref,
                 *, n_steps, aw, bw, hda, dkb):
    h = _rms(x_ref[...], g_ref[...]).astype(BF16)

    def proj(c0, width):
        return jnp.dot(h, w_ref[:, c0:c0 + width], preferred_element_type=F32)

    def qk_norm_rot(z, gain_ref, out_ref):
        ss = jnp.dot((z * z).astype(BF16), gm_ref[...], preferred_element_type=F32)
        y = z * lax.rsqrt(ss + EPS) * gain_ref[...]
        for c in range(aw // LANE):
            yc = y[:, c * LANE:(c + 1) * LANE]
            r = yc * ca_ref[...] + _rot_half(yc, hda // 2) * sa_ref[...]
            out_ref[:, c * LANE:(c + 1) * LANE] = r.astype(out_ref.dtype)

    def rot_b(z, out_ref, scale):
        for c in range(bw // LANE):
            zc = z[:, c * LANE:(c + 1) * LANE]
            r = zc * cb_ref[...] + _rot_half(zc, dkb // 2) * sb_ref[...]
            out_ref[:, c * LANE:(c + 1) * LANE] = (r * scale).astype(out_ref.dtype)

    qk_norm_rot(proj(0, aw), gq_ref, qa_ref)
    qk_norm_rot(proj(aw, aw), gk_ref, ka_ref)
    va = proj(2 * aw, aw)
    vabf_ref[...] = va.astype(BF16)

    def fill(slot):
        vbuf_ref[slot] = va
    _emit_rows(vbuf_ref, va_ref, vsem_ref, n_steps, fill)
    rot_b(proj(3 * aw, bw), qb_ref, 1.0)
    rot_b(proj(3 * aw + bw, bw), kb_ref, dkb ** -0.5)
    vb_ref[...] = proj(3 * aw + 2 * bw, bw).astype(BF16)
    gb_ref[...] = proj(3 * aw + 3 * bw, bw).astype(BF16)


def _project(x2d, seq, gain, w_bf, gm, gq_t, gk_t, tabs, *, aw, bw, hda, dkb):
    t, d = x2d.shape[0], gain.shape[1]
    ts = min(TOKEN_TILE, seq)
    n_s = seq // ts
    row = lambda i: (i, 0)
    const = lambda i: (0, 0)
    tab = lambda i: (i % n_s, 0)
    tab_spec = pl.BlockSpec((ts, LANE), tab)
    out_a = pl.BlockSpec((ts, aw), row)
    out_b = pl.BlockSpec((ts, bw), row)
    return pl.pallas_call(
        functools.partial(_proj_kernel, n_steps=t // ts, aw=aw, bw=bw, hda=hda, dkb=dkb),
        grid=(t // ts,),
        in_specs=[pl.BlockSpec((ts, d), row), pl.BlockSpec((1, d), const),
                  pl.BlockSpec(w_bf.shape, const), pl.BlockSpec(gm.shape, const),
                  pl.BlockSpec((1, aw), const), pl.BlockSpec((1, aw), const),
                  tab_spec, tab_spec, tab_spec, tab_spec],
        out_specs=[out_a, out_a, pl.BlockSpec(memory_space=pl.ANY), out_a, out_b, out_b, out_b, out_b],
        out_shape=[jax.ShapeDtypeStruct((t, aw), BF16), jax.ShapeDtypeStruct((t, aw), F32),
                   jax.ShapeDtypeStruct((t, aw // LANE, LANE), F32), jax.ShapeDtypeStruct((t, aw), BF16),
                   jax.ShapeDtypeStruct((t, bw), BF16), jax.ShapeDtypeStruct((t, bw), BF16),
                   jax.ShapeDtypeStruct((t, bw), BF16), jax.ShapeDtypeStruct((t, bw), BF16)],
        scratch_shapes=[pltpu.VMEM((2, ts, aw), F32), pltpu.SemaphoreType.DMA((2,))],
        compiler_params=_cparams("arbitrary"),
        name="proj",
    )(x2d, gain, w_bf, gm, gq_t, gk_t, *tabs)


def _scaled_queries(q, hda):
    return (q.astype(F32) * (hda ** -0.5 * LOG2E)).astype(BF16)


def _split_halves(k, hda):
    lo = (lax.broadcasted_iota(I32, k.shape, 1) & (LANE - 1)) < hda
    return jnp.where(lo, k, 0.0).astype(BF16), jnp.where(lo, 0.0, k).astype(BF16)


def _reset_softmax_state(m_ref, l_ref, acc_ref):
    m_ref[...] = jnp.full(m_ref.shape, NEG, F32)
    l_ref[...] = jnp.zeros(l_ref.shape, F32)
    acc_ref[...] = jnp.zeros(acc_ref.shape, F32)


def _softmax_blocks(qs, blocks, m_ref, l_ref, acc_ref):
    scores = [[lax.dot_general(kk, q, (((1,), (1,)), ((), ())), preferred_element_type=F32)
               for q, halves in zip(qs, k_halves) for kk in halves] for k_halves, _, _ in blocks]
    for (_, vts, valid), block_scores in zip(blocks, scores):
        probs, alphas = [], []
        for c, s in enumerate(block_scores):
            if valid is not None:
                s = jnp.where(valid, s, NEG)
            m_old = m_ref[c]
            m_new = jnp.maximum(m_old, jnp.max(s, axis=0, keepdims=True))
            alpha = jnp.exp2(m_old - m_new)
            p = jnp.exp2(s - m_new)
            l_ref[c] = alpha * l_ref[c] + jnp.sum(p, axis=0, keepdims=True)
            m_ref[c] = m_new
            probs.append(p.astype(BF16))
            alphas.append(alpha)
        for c, (p, alpha) in enumerate(zip(probs, alphas)):
            acc_ref[c] = alpha * acc_ref[c] + jnp.dot(vts[c // 2], p, preferred_element_type=F32)


def _diff_lambda(lv_ref, lam_init):
    lv = lv_ref[...]
    return (jnp.exp(jnp.sum(lv[0:1] * lv[1:2], axis=1, keepdims=True))
            - jnp.exp(jnp.sum(lv[2:3] * lv[3:4], axis=1, keepdims=True)) + lam_init)


def _finish_heads(lv_ref, sg_ref, o_ref, l_ref, acc_ref, lam_init):
    lam = _diff_lambda(lv_ref, lam_init)
    for h in range(o_ref.shape[2] // LANE):
        o_t = acc_ref[2 * h] / l_ref[2 * h] - lam * (acc_ref[2 * h + 1] / l_ref[2 * h + 1])
        o_ref[0, :, h * LANE:(h + 1) * LANE] = (_rms(o_t.T, sg_ref[...]) * (1.0 - lam_init)).astype(o_ref.dtype)


def _attn_prompt_kernel(q_ref, k_ref, v_ref, lv_ref, sg_ref, o_ref,
                        qs_ref, klo_ref, khi_ref, vt_ref, m_ref, l_ref, acc_ref, *, tile, hda, lam_init):
    qi = pl.program_id(1)
    n_pairs = q_ref.shape[2] // LANE
    n_blk = k_ref.shape[1] // tile

    @pl.when(qi == 0)
    def _stage_keys():
        def stage(j, carry):
            rows = pl.ds(pl.multiple_of(j * tile, tile), tile)
            klo_ref[rows, :], khi_ref[rows, :] = _split_halves(k_ref[0, rows, :], hda)
            v = v_ref[0, rows, :]
            for h in range(n_pairs):
                vt_ref[j * n_pairs + h] = v[:, h * LANE:(h + 1) * LANE].astype(F32).T.astype(BF16)
            return carry
        lax.fori_loop(0, n_blk, stage, 0)

    qs_ref[...] = _scaled_queries(q_ref[0], hda)
    _reset_softmax_state(m_ref, l_ref, acc_ref)

    cols = [slice(h * LANE, (h + 1) * LANE) for h in range(n_pairs)]

    def key_block(kb, valid):
        rows = pl.ds(pl.multiple_of(kb * tile, tile), tile)
        return ([(klo_ref[rows, c], khi_ref[rows, c]) for c in cols],
                [vt_ref[kb * n_pairs + h] for h in range(n_pairs)], valid)

    def attend(blocks):
        _softmax_blocks([qs_ref[:, c] for c in cols], blocks, m_ref, l_ref, acc_ref)

    def past_pair(j, carry):
        attend([key_block(2 * j, None), key_block(2 * j + 1, None)])
        return carry
    lax.fori_loop(0, qi // 2, past_pair, 0)
    shift = CHUNK.bit_length() - 1
    kchunk = lax.broadcasted_iota(I32, (tile, tile), 0) >> shift
    qchunk = lax.broadcasted_iota(I32, (tile, tile), 1) >> shift
    own = kchunk <= qchunk

    @pl.when(qi % 2 == 1)
    def _odd():
        attend([key_block(qi - 1, None), key_block(qi, own)])

    @pl.when(qi % 2 == 0)
    def _even():
        attend([key_block(qi, own)])
    _finish_heads(lv_ref, sg_ref, o_ref, l_ref, acc_ref, lam_init)


def _attn_cached_kernel(q_ref, kct_ref, vc_ref, kn_ref, vn_ref, lv_ref, sg_ref, o_ref,
                        qlo_ref, qhi_ref, m_ref, l_ref, acc_ref, *, hda, lam_init):
    pi = pl.program_id(1)
    n_pairs = q_ref.shape[2] // LANE
    cols = [slice(h * LANE, (h + 1) * LANE) for h in range(n_pairs)]

    @pl.when(pi == 0)
    def _start():
        qlo_ref[...], qhi_ref[...] = _split_halves(_scaled_queries(q_ref[0], hda), hda)
        _reset_softmax_state(m_ref, l_ref, acc_ref)

    def update(c, s, v):
        m_old = m_ref[c]
        m_new = jnp.maximum(m_old, jnp.max(s, axis=1, keepdims=True))
        alpha = jnp.exp2(m_old - m_new)
        p = jnp.exp2(s - m_new)
        l_ref[c] = alpha * l_ref[c] + jnp.sum(p, axis=1, keepdims=True)
        acc_ref[c] = alpha * acc_ref[c] + jnp.dot(p.astype(BF16), v, preferred_element_type=F32)
        m_ref[c] = m_new

    def attend(score_fn, value_fn):
        scores = [score_fn(q[:, c], h) for h, c in enumerate(cols) for q in (qlo_ref, qhi_ref)]
        for h in range(n_pairs):
            v = value_fn(h)
            update(2 * h, scores[2 * h], v)
            update(2 * h + 1, scores[2 * h + 1], v)

    attend(lambda q, h: jnp.dot(q, kct_ref[0, cols[h], :].astype(BF16), preferred_element_type=F32),
           lambda h: vc_ref[0, :, h, :].astype(BF16))

    @pl.when(pi == pl.num_programs(1) - 1)
    def _finish():
        attend(lambda q, h: lax.dot_general(q, kn_ref[0, :, cols[h]].astype(BF16), (((1,), (1,)), ((), ())),
                                            preferred_element_type=F32),
               lambda h: vn_ref[0, :, cols[h]])
        lam = _diff_lambda(lv_ref, lam_init)
        for h, c in enumerate(cols):
            o = acc_ref[2 * h] / l_ref[2 * h] - lam * (acc_ref[2 * h + 1] / l_ref[2 * h + 1])
            o_ref[0, :, c] = (_rms(o, sg_ref[...]) * (1.0 - lam_init)).astype(o_ref.dtype)


def _diff_attention(qa, k_all, k_layer, va, cache, lam_vec, subln, *, causal, hda, lam_init):
    b, s, w = qa.shape
    n_chains = 2 * (w // LANE)
    const = lambda bi, si: (0, 0)
    whole = pl.BlockSpec((1, s, w), lambda bi, si: (bi, 0, 0))
    whole_k = pl.BlockSpec((None, 1, s, w), lambda bi, si: (k_layer, bi, 0, 0))
    small = [pl.BlockSpec(lam_vec.shape, const), pl.BlockSpec((1, LANE), const)]

    def state(tq):
        return [pltpu.VMEM((tq, w), BF16), pltpu.VMEM((n_chains, 1, tq), F32),
                pltpu.VMEM((n_chains, 1, tq), F32), pltpu.VMEM((n_chains, LANE, tq), F32)]

    if causal:
        tile = min(ATTN_TILE, s)
        q_blk = pl.BlockSpec((1, tile, w), lambda bi, qi: (bi, qi, 0))
        qs, *softmax_state = state(tile)
        return pl.pallas_call(
            functools.partial(_attn_prompt_kernel, tile=tile, hda=hda, lam_init=lam_init),
            grid=(b, s // tile),
            in_specs=[q_blk, whole_k, whole] + small,
            out_specs=q_blk,
            out_shape=jax.ShapeDtypeStruct((b, s, w), BF16),
            scratch_shapes=[qs, pltpu.VMEM((s, w), BF16), pltpu.VMEM((s, w), BF16),
                            pltpu.VMEM((s // tile * (w // LANE), LANE, tile), BF16)] + softmax_state,
            compiler_params=_cparams("parallel", "arbitrary"),
            name="diff_attn",
        )(qa, k_all, va, lam_vec, subln)
    ckt, cv, li = cache
    past = ckt.shape[3]
    tkc = min(CACHE_TILE, past)
    return pl.pallas_call(
        functools.partial(_attn_cached_kernel, hda=hda, lam_init=lam_init),
        grid=(b, past // tkc),
        in_specs=[whole, pl.BlockSpec((None, 1, w, tkc), lambda bi, pi: (li, bi, 0, pi)),
                  pl.BlockSpec((None, 1, tkc, *cv.shape[3:]), lambda bi, pi: (li, bi, pi, 0, 0)),
                  whole_k, whole] + small,
        out_specs=whole,
        out_shape=jax.ShapeDtypeStruct((b, s, w), BF16),
        scratch_shapes=[pltpu.VMEM((s, w), BF16), pltpu.VMEM((s, w), BF16), pltpu.VMEM((n_chains, s, 1), F32),
                        pltpu.VMEM((n_chains, s, 1), F32), pltpu.VMEM((n_chains, s, LANE), F32)],
        compiler_params=_cparams("parallel", "arbitrary"),
        name="diff_attn_cached",
    )(qa, ckt, cv, k_all, va, lam_vec, subln)


def _retention_kernel(*refs, has_state, chunk):
    if has_state:
        q_ref, k_ref, v_ref, g_ref, rg_ref, s0_ref, y_ref, so_ref, st_ref, decay_ref = refs
    else:
        q_ref, k_ref, v_ref, g_ref, rg_ref, y_ref, so_ref, st_ref, decay_ref = refs
    ci = pl.program_id(1)
    n_heads = q_ref.shape[2] // LANE
    log_gammas = [float(np.log(1.0 - 2.0 ** (-5.0 - h))) for h in range(n_heads)]

    @pl.when(ci == 0)
    def _init():
        st_ref[...] = s0_ref[0] if has_state else jnp.zeros(st_ref.shape, F32)
        row = lax.broadcasted_iota(I32, (chunk, chunk), 0)
        col = lax.broadcasted_iota(I32, (chunk, chunk), 1)
        dist = (row - col).astype(F32)
        for h, lg in enumerate(log_gammas):
            decay_ref[h] = jnp.where(dist >= 0, jnp.exp(lg * jnp.maximum(dist, 0.0)), 0.0)

    idx = lax.broadcasted_iota(I32, (chunk, 1), 0).astype(F32)
    heads = list(enumerate(log_gammas))
    cols = [slice(h * LANE, (h + 1) * LANE) for h in range(n_heads)]
    qk = [lax.dot_general(q_ref[0, :, c], k_ref[0, :, c], (((1,), (1,)), ((), ())), preferred_element_type=F32)
          for c in cols]
    cross = [jnp.dot(q_ref[0, :, c], st_ref[h].astype(BF16), preferred_element_type=F32)
             for h, c in enumerate(cols)]
    kv = []
    for (h, lg), c in zip(heads, cols):
        kz = (k_ref[0, :, c].astype(F32) * jnp.exp(lg * (chunk - 1.0 - idx))).astype(BF16)
        kv.append(lax.dot_general(kz, v_ref[0, :, c], (((0,), (0,)), ((), ())), preferred_element_type=F32))
    intra = [jnp.dot((qk[h] * decay_ref[h]).astype(BF16), v_ref[0, :, c], preferred_element_type=F32)
             for h, c in enumerate(cols)]
    for (h, lg), c in zip(heads, cols):
        out = intra[h] + cross[h] * jnp.exp(lg * (idx + 1.0))
        st_ref[h] = float(np.exp(lg * chunk)) * st_ref[h] + kv[h]
        gate = g_ref[0, :, c].astype(F32)
        y_ref[0, :, c] = (_rms(out, rg_ref[...]) * (gate * jax.nn.sigmoid(gate))).astype(y_ref.dtype)

    @pl.when(ci == pl.num_programs(1) - 1)
    def _emit_state():
        so_ref[0] = st_ref[...]


def _retention(qb, kb, vb, gb, ret_g, state0):
    b, s, w = qb.shape
    n_heads = w // LANE
    chunk = min(RET_CHUNK, s)
    has_state = state0 is not None
    blk = pl.BlockSpec((1, chunk, w), lambda bi, ci: (bi, ci, 0))
    st_spec = pl.BlockSpec((1, n_heads, LANE, LANE), lambda bi, ci: (bi, 0, 0, 0))
    in_specs = [blk, blk, blk, blk, pl.BlockSpec((1, LANE), lambda bi, ci: (0, 0))]
    args = [qb, kb, vb, gb, ret_g]
    if has_state:
        in_specs.append(st_spec)
        args.append(state0)
    return pl.pallas_call(
        functools.partial(_retention_kernel, has_state=has_state, chunk=chunk),
        grid=(b, s // chunk),
        in_specs=in_specs,
        out_specs=[blk, st_spec],
        out_shape=[jax.ShapeDtypeStruct((b, s, w), BF16), jax.ShapeDtypeStruct((b, n_heads, LANE, LANE), F32)],
        scratch_shapes=[pltpu.VMEM((n_heads, LANE, LANE), F32), pltpu.VMEM((n_heads, chunk, chunk), F32)],
        compiler_params=_cparams("parallel", "arbitrary"),
        name="retention",
    )(*args)


def _router_logits(h_bf, wr_ref, br_ref):
    return jnp.dot(h_bf, wr_ref[...], preferred_element_type=F32) + br_ref[...]


def _first_row_of_max(vals, row):
    vmax = jnp.max(vals, axis=0, keepdims=True)
    first = jnp.min(jnp.where(vals == vmax, row.astype(F32), float(ROUTER_LANES)), axis=0, keepdims=True)
    return first.astype(I32)


def _mix_kernel(x_ref, ya_ref, yb_ref, woa_ref, wob_ref, g_ref, wrt_ref, brt_ref,
                x1_ref, cls_ref, cnt_ref, buf_ref, sem_ref, *, n_steps, n_groups, n_experts):
    x1 = (x_ref[...] + jnp.dot(ya_ref[...], woa_ref[...], preferred_element_type=F32)
          + jnp.dot(yb_ref[...], wob_ref[...], preferred_element_type=F32))

    def fill(slot):
        buf_ref[slot] = x1
    _emit_rows(buf_ref, x1_ref, sem_ref, n_steps, fill)
    h = _rms(x1, g_ref[...]).astype(BF16)
    logits = lax.dot_general(wrt_ref[...], h, (((1,), (1,)), ((), ())), preferred_element_type=F32) + brt_ref[...]
    row = lax.broadcasted_iota(I32, logits.shape, 0)
    g_sel = _first_row_of_max(jnp.where(row < n_groups, logits, NEG), row)
    e_lo = n_groups + g_sel * n_experts
    e_logits = jnp.where((row >= e_lo) & (row < e_lo + n_experts), logits, NEG)
    top1 = _first_row_of_max(e_logits, row)
    top2 = _first_row_of_max(jnp.where(row == top1, NEG, e_logits), row)
    ea = jnp.minimum(top1, top2) - e_lo
    eb = jnp.maximum(top1, top2) - e_lo
    pair = ((ea * (2 * n_experts - 1 - ea)) >> 1) + (eb - ea - 1)
    n_pairs = n_experts * (n_experts - 1) // 2
    cls = g_sel * n_pairs + pair
    cls_ref[0] = cls

    @pl.when(pl.program_id(0) == 0)
    def _zero():
        cnt_ref[...] = jnp.zeros(cnt_ref.shape, F32)

    cnt_ref[...] += jnp.sum(jnp.where(row == cls, 1.0, 0.0), axis=1, keepdims=True)


def _mix(x2d, ya, yb, wo_a, wo_b, gain, wrt, brt, *, n_groups, n_experts):
    t, d = x2d.shape[0], gain.shape[1]
    ts = min(TOKEN_TILE, t)
    row = lambda i: (i, 0)
    const = lambda i: (0, 0)
    x1_shape = _row_tiled(t, d)
    return pl.pallas_call(
        functools.partial(_mix_kernel, n_steps=t // ts, n_groups=n_groups, n_experts=n_experts),
        grid=(t // ts,),
        in_specs=[pl.BlockSpec((ts, d), row), pl.BlockSpec((ts, ya.shape[1]), row),
                  pl.BlockSpec((ts, yb.shape[1]), row), pl.BlockSpec(wo_a.shape, const),
                  pl.BlockSpec(wo_b.shape, const), pl.BlockSpec((1, d), const),
                  pl.BlockSpec(wrt.shape, const), pl.BlockSpec(brt.shape, const)],
        out_specs=[pl.BlockSpec(memory_space=pl.ANY), pl.BlockSpec((1, 1, ts), lambda i: (i, 0, 0)),
                   pl.BlockSpec((ROUTER_LANES, 1), const)],
        out_shape=[jax.ShapeDtypeStruct(x1_shape, F32), jax.ShapeDtypeStruct((t // ts, 1, ts), I32),
                   jax.ShapeDtypeStruct((ROUTER_LANES, 1), F32)],
        scratch_shapes=[pltpu.VMEM((2, ts, d), F32), pltpu.SemaphoreType.DMA((2,))],
        compiler_params=_cparams("arbitrary"),
        name="mix",
    )(x2d, ya, yb, wo_a, wo_b, gain, wrt, brt)


ISSUE_UNROLL = 8
DMA_LANES = 2


def _row_copy(src, src_row, dst, dst_row, sem):
    return pltpu.make_async_copy(src.at[src_row], dst.at[dst_row], sem)


def _rows_copy(src, dst, n, sem):
    return pltpu.make_async_copy(src.at[pl.ds(0, n)], dst.at[pl.ds(0, n)], sem)


def _slot_kernel(cls_ref, start_ref, pos_ref, next_ref, earlier_ref):
    ts = cls_ref.shape[2]

    @pl.when(pl.program_id(0) == 0)
    def _init():
        next_ref[...] = start_ref[...]
        before = lax.broadcasted_iota(I32, (ts, ts), 0) < lax.broadcasted_iota(I32, (ts, ts), 1)
        earlier_ref[...] = jnp.where(before, 1.0, 0.0).astype(BF16)

    member = lax.broadcasted_iota(I32, (ROUTER_LANES, ts), 0) == cls_ref[0]
    onehot = jnp.where(member, 1.0, 0.0)
    ahead = jnp.dot(onehot.astype(BF16), earlier_ref[...], preferred_element_type=F32)
    pos_ref[0] = jnp.sum(onehot * (ahead + next_ref[...]), axis=0, keepdims=True).astype(I32)
    next_ref[...] += jnp.sum(onehot, axis=1, keepdims=True)


def _slots(cls, slot_start):
    n_tiles, _, ts = cls.shape
    blk = pl.BlockSpec((1, 1, ts), lambda i: (i, 0, 0))
    return pl.pallas_call(
        _slot_kernel,
        grid=(n_tiles,),
        in_specs=[blk, pl.BlockSpec((ROUTER_LANES, 1), lambda i: (0, 0))],
        out_specs=blk,
        out_shape=jax.ShapeDtypeStruct(cls.shape, I32),
        scratch_shapes=[pltpu.VMEM((ROUTER_LANES, 1), F32), pltpu.VMEM((ts, ts), BF16)],
        compiler_params=_cparams("arbitrary"),
        name="slots",
    )(cls, slot_start.astype(F32).reshape(ROUTER_LANES, 1))


def _dispatch_kernel(fill_ref, end_ref, pos_ref, x_ref, xs_ref, zero_ref, sem, *, n_cls):
    step = pl.program_id(0)
    ts = x_ref.shape[0]

    @pl.when(step == 0)
    def _init():
        zero_ref[...] = jnp.zeros(zero_ref.shape, F32)

    def place(j, carry):
        for lane in range(DMA_LANES):
            r = j * DMA_LANES + lane
            _row_copy(x_ref, r, xs_ref, pos_ref[0, 0, r], sem).start(priority=lane)
        return carry
    lax.fori_loop(0, ts // DMA_LANES, place, 0, unroll=ISSUE_UNROLL // DMA_LANES)
    _rows_copy(x_ref, xs_ref, ts, sem).wait()

    @pl.when(step == pl.num_programs(0) - 1)
    def _pad():
        tile = zero_ref.shape[0]

        def pad_copies(c, act):
            row, n = fill_ref[c], end_ref[c] - fill_ref[c]
            p = tile // 2
            while p:
                piece = pltpu.make_async_copy(zero_ref.at[pl.ds(0, p)], xs_ref.at[pl.ds(row, p)], sem)
                pl.when((n & p) != 0)(functools.partial(act, piece))
                row = row + (n & p)
                p //= 2

        def start_class(c, carry):
            pad_copies(c, lambda piece: piece.start())
            return carry
        lax.fori_loop(0, n_cls, start_class, 0)

        def wait_class(c, carry):
            pad_copies(c, lambda piece: piece.wait())
            return carry
        lax.fori_loop(0, n_cls, wait_class, 0)

        first_free = lax.div(end_ref[n_cls - 1], tile)
        n_tiles = xs_ref.shape[0] // tile

        def tile_copy(ti):
            return pltpu.make_async_copy(zero_ref, xs_ref.at[pl.ds(pl.multiple_of(ti * tile, tile), tile)], sem)

        def clear(ti, carry):
            tile_copy(ti).start()
            return carry
        lax.fori_loop(first_free, n_tiles, clear, 0)

        def clear_wait(ti, carry):
            tile_copy(ti).wait()
            return carry
        lax.fori_loop(first_free, n_tiles, clear_wait, 0)


def _dispatch(x1, pos, fill_from, slot_end, n_slots, *, n_cls, tile):
    n_tiles, _, ts = pos.shape
    return pl.pallas_call(
        functools.partial(_dispatch_kernel, n_cls=n_cls),
        grid_spec=pltpu.PrefetchScalarGridSpec(
            num_scalar_prefetch=2,
            grid=(n_tiles,),
            in_specs=[pl.BlockSpec((1, 1, ts), lambda i, *_: (i, 0, 0), memory_space=pltpu.SMEM),
                      _rows_spec(ts, x1.shape)],
            out_specs=pl.BlockSpec(memory_space=pl.ANY),
            scratch_shapes=[pltpu.VMEM((tile, *x1.shape[1:]), F32), pltpu.SemaphoreType.DMA],
        ),
        out_shape=jax.ShapeDtypeStruct((n_slots, *x1.shape[1:]), F32),
        compiler_params=_cparams("arbitrary"),
        name="dispatch",
    )(fill_from, slot_end, pos, x1)


def _gather_rows(pos_ref, ys_ref, rows_ref, sem):
    ts = rows_ref.shape[0]

    def fetch(j, carry):
        for lane in range(DMA_LANES):
            r = j * DMA_LANES + lane
            _row_copy(ys_ref, pos_ref[0, 0, r], rows_ref, r, sem).start(priority=lane)
        return carry
    lax.fori_loop(0, ts // DMA_LANES, fetch, 0, unroll=ISSUE_UNROLL // DMA_LANES)
    _rows_copy(ys_ref, rows_ref, ts, sem).wait()


def _unpermute_kernel(pos_ref, ys_ref, o_ref, stage_ref, sem, out_sem, *, n_steps):
    def fill(slot):
        _gather_rows(pos_ref, ys_ref, stage_ref.at[slot], sem)
    _emit_rows(stage_ref, o_ref, out_sem, n_steps, fill)


def _unpermute(ys, pos, t):
    n_tiles, _, ts = pos.shape
    hbm = pl.BlockSpec(memory_space=pl.ANY)
    return pl.pallas_call(
        functools.partial(_unpermute_kernel, n_steps=n_tiles),
        grid=(n_tiles,),
        in_specs=[pl.BlockSpec((1, 1, ts), lambda i: (i, 0, 0), memory_space=pltpu.SMEM), hbm],
        out_specs=hbm,
        out_shape=jax.ShapeDtypeStruct((t, ys.shape[1] * ys.shape[2]), F32),
        scratch_shapes=[pltpu.VMEM((2, ts, *ys.shape[1:]), F32), pltpu.SemaphoreType.DMA,
                        pltpu.SemaphoreType.DMA((2,))],
        compiler_params=_cparams("arbitrary"),
        name="unpermute",
    )(pos, ys)


def _expert_kernel(tg_ref, ta_ref, tb_ref, nv_ref, xs_ref, g_ref, wr_ref, br_ref,
                   wga_ref, wua_ref, wda_ref, wgb_ref, wub_ref, wdb_ref, ys_ref,
                   xbuf_ref, in_sem_ref, buf_ref, sem_ref, *, n_steps, n_groups, n_experts):
    i = pl.program_id(0)
    in_slot = _fetch_rows(xs_ref, xbuf_ref, in_sem_ref, n_steps)

    def run():
        x = xbuf_ref[in_slot]
        h = _rms(x, g_ref[...]).astype(BF16)
        logits = _router_logits(h, wr_ref, br_ref)
        gates = [jnp.dot(h, w[0, 0], preferred_element_type=F32) for w in (wga_ref, wgb_ref)]
        ups = [jnp.dot(h, w[0, 0], preferred_element_type=F32) for w in (wua_ref, wub_ref)]
        lane = lax.broadcasted_iota(I32, logits.shape, 1)
        grp, ea, eb = tg_ref[i], ta_ref[i], tb_ref[i]

        def pick(idx):
            return jnp.sum(jnp.where(lane == idx, logits, 0.0), axis=1, keepdims=True)

        g_logits = jnp.where(lane < n_groups, logits, NEG)
        g_max = jnp.max(g_logits, axis=1, keepdims=True)
        g_w = jnp.exp(pick(grp) - g_max) / jnp.sum(jnp.exp(g_logits - g_max), axis=1, keepdims=True)
        e_lo = n_groups + grp * n_experts
        w_a = jax.nn.sigmoid(pick(e_lo + ea) - pick(e_lo + eb))
        acts = [(gate * jax.nn.sigmoid(gate) * up * comb).astype(BF16)
                for gate, up, comb in zip(gates, ups, (g_w * w_a, g_w * (1.0 - w_a)))]
        return (x + jnp.dot(acts[0], wda_ref[0, 0], preferred_element_type=F32)
                + jnp.dot(acts[1], wdb_ref[0, 0], preferred_element_type=F32))

    def fill(slot):
        @pl.when(nv_ref[i] == 0)
        def _unused_tile():
            buf_ref[slot] = jnp.zeros(buf_ref.shape[1:], F32)

        @pl.when(nv_ref[i] > 0)
        def _used_tile():
            buf_ref[slot] = run()
    _emit_rows(buf_ref, ys_ref, sem_ref, n_steps, fill)


def _experts(xs, gain, wr, br, wg, wu, wd, meta, *, tile, n_groups, n_experts):
    n_slots, d = xs.shape[0], gain.shape[1]
    de = wg.shape[-1]
    n_steps = n_slots // tile
    tile_g, tile_a, tile_b, tile_nv = meta
    const = lambda i, *_: (0, 0)
    sel_a = lambda i, tg, ta, tb, nv: (tg[i], ta[i], 0, 0)
    sel_b = lambda i, tg, ta, tb, nv: (tg[i], tb[i], 0, 0)
    up_spec_a = pl.BlockSpec((1, 1, d, de), sel_a)
    up_spec_b = pl.BlockSpec((1, 1, d, de), sel_b)
    hbm = pl.BlockSpec(memory_space=pl.ANY)
    return pl.pallas_call(
        functools.partial(_expert_kernel, n_steps=n_steps, n_groups=n_groups, n_experts=n_experts),
        grid_spec=pltpu.PrefetchScalarGridSpec(
            num_scalar_prefetch=4,
            grid=(n_steps,),
            in_specs=[hbm, pl.BlockSpec((1, d), const),
                      pl.BlockSpec(wr.shape, const), pl.BlockSpec(br.shape, const),
                      up_spec_a, up_spec_a, pl.BlockSpec((1, 1, de, d), sel_a),
                      up_spec_b, up_spec_b, pl.BlockSpec((1, 1, de, d), sel_b)],
            out_specs=hbm,
            scratch_shapes=[pltpu.VMEM((2, tile, d), F32), pltpu.SemaphoreType.DMA((2,)),
                            pltpu.VMEM((2, tile, d), F32), pltpu.SemaphoreType.DMA((2,))],
        ),
        out_shape=jax.ShapeDtypeStruct(xs.shape, F32),
        compiler_params=_cparams("arbitrary"),
        name="experts",
    )(tile_g, tile_a, tile_b, tile_nv, xs, gain, wr, br, wg, wu, wd, wg, wu, wd)


def _class_layout(counts, n_cls, n_pairs, n_experts, n_tiles, tile):
    counts = counts[:n_cls, 0].astype(I32)
    tiles_c = (counts + tile - 1) // tile
    tile_end = jnp.cumsum(tiles_c)
    tile_start = tile_end - tiles_c
    n_active = tile_end[-1]
    pad = ROUTER_LANES - n_cls
    slot_start = jnp.pad(tile_start * tile, (0, pad))
    slot_end = jnp.pad(tile_end * tile, (0, pad))
    fill_from = jnp.pad(tile_start * tile + counts, (0, pad))
    t_idx = jnp.arange(n_tiles, dtype=I32)
    t_cls = jnp.minimum(jnp.sum((tile_end[None, :] <= t_idx[:, None]).astype(I32), axis=1), n_cls - 1)
    active = t_idx < n_active
    t_nv = jnp.where(active, jnp.clip(counts[t_cls] - (t_idx - tile_start[t_cls]) * tile, 0, tile), 0)
    t_cls = jnp.where(active, t_cls, t_cls[jnp.maximum(n_active - 1, 0)])
    pair_a, pair_b = np.triu_indices(n_experts, 1)
    pair = t_cls % n_pairs
    meta = (t_cls // n_pairs, jnp.asarray(pair_a, I32)[pair], jnp.asarray(pair_b, I32)[pair], t_nv.astype(I32))
    return slot_start.astype(I32), fill_from.astype(I32), slot_end.astype(I32), meta


def _rope_tables(pos, group, signed_half):
    half = group // 2
    lane = np.arange(LANE)
    inv = (ROPE_THETA ** (-jnp.arange(0, group, 2, dtype=F32) / group))[lane % half]
    ang = pos.astype(F32)[:, None] * inv[None, :]
    sign = np.where((lane % group) < signed_half, -1.0, 1.0).astype(np.float32)
    return jnp.cos(ang), jnp.sin(ang) * sign[None, :]


def _layer(x2d, b, s, pos, cache, state0, p, li, dims):
    aw, bw, hda, dkb = dims["aw"], dims["bw"], dims["hda"], dims["dkb"]
    n_groups, n_experts = dims["n_groups"], dims["n_experts"]
    lam_init = 0.8 - 0.6 * float(np.exp(-0.3 * li))
    t = b * s
    tabs = (*_rope_tables(pos, hda, hda // 2), *_rope_tables(pos, dkb, dkb // 2))
    qa, ka, va, va_bf, qb, kb, vb, gb = _project(x2d, s, p["norm_attn"], p["w_in"], p["gm"], p["gq"], p["gk"], tabs,
                                                 aw=aw, bw=bw, hda=hda, dkb=dkb)
    shape3 = lambda a: a.reshape(b, s, a.shape[-1])
    ya = _diff_attention(shape3(qa), ka.reshape(1, b, s, aw), 0, shape3(va_bf), cache,
                         p["lam_vec"], p["subln"], causal=cache is None, hda=hda, lam_init=lam_init)
    yb, new_state = _retention(shape3(qb), shape3(kb), shape3(vb), shape3(gb), p["ret_norm"], state0)
    x1, cls, counts = _mix(x2d, ya.reshape(t, aw), yb.reshape(t, bw), p["wo_a"], p["wo_b"], p["norm_ffn"],
                           p["wrt"], p["brt"], n_groups=n_groups, n_experts=n_experts)
    n_pairs = n_experts * (n_experts - 1) // 2
    n_cls = n_groups * n_pairs
    tile = EXPERT_TILE if t >= EXPERT_TILE * n_cls else SMALL_EXPERT_TILE
    n_tiles = t // tile + n_cls
    slot_start, fill_from, slot_end, meta = _class_layout(counts, n_cls, n_pairs, n_experts, n_tiles, tile)
    slot_of = _slots(cls, slot_start)
    xs = _dispatch(x1, slot_of, fill_from, slot_end, n_tiles * tile, n_cls=n_cls, tile=tile)
    ys = _experts(xs, p["norm_ffn"], p["wr"], p["br"], p["w_gate"], p["w_up"], p["w_down"], meta,
                  tile=tile, n_groups=n_groups, n_experts=n_experts)
    return _unpermute(ys, slot_of, t), ka, va, new_state


def kernel(x_prompt, x_sample, cache_k, cache_v, state_ret, norm_attn, w_in, q_norm, k_norm, lam_vec,
           subln, ret_norm, w_out, norm_ffn, w_group, b_group, w_expert, b_expert, w_gate, w_up, w_down):
    depth, dec_b, past, heads2, hda = cache_k.shape
    _, _, n_ret, dkb, dvb = state_ret.shape
    n_groups, n_experts = w_gate.shape[1], w_gate.shape[2]
    aw, bw = heads2 * hda, n_ret * dkb
    assert 2 * hda == LANE and dkb == LANE and dvb == LANE and aw % LANE == 0
    assert n_groups * (1 + n_experts) <= ROUTER_LANES
    assert n_groups * n_experts * (n_experts - 1) // 2 <= ROUTER_LANES
    dims = dict(aw=aw, bw=bw, hda=hda, dkb=dkb, n_groups=n_groups, n_experts=n_experts)
    d = x_prompt.shape[-1]
    group_of = np.arange(aw) // hda
    gm = jnp.asarray((group_of[:, None] == group_of[None, :]) / hda, BF16)
    row = lambda v: v.reshape(1, -1).astype(F32)

    def layer_params(li):
        wr = jnp.concatenate([w_group[li], w_expert[li].reshape(d, n_groups * n_experts)], axis=1)
        br = jnp.concatenate([b_group[li], b_expert[li].reshape(-1)])
        lane_pad = ROUTER_LANES - wr.shape[1]
        return dict(
            norm_attn=row(norm_attn[li]), w_in=w_in[li].astype(BF16), gm=gm,
            gq=row(jnp.tile(q_norm[li], heads2)), gk=row(jnp.tile(k_norm[li], heads2)),
            lam_vec=lam_vec[li].astype(F32), subln=row(subln[li]), ret_norm=row(ret_norm[li]),
            wo_a=w_out[li, :aw].astype(BF16), wo_b=w_out[li, aw:].astype(BF16), norm_ffn=row(norm_ffn[li]),
            wr=jnp.pad(wr, ((0, 0), (0, lane_pad))).astype(BF16), br=row(jnp.pad(br, (0, lane_pad))),
            wrt=jnp.pad(wr, ((0, 0), (0, lane_pad))).T.astype(BF16),
            brt=jnp.pad(br, (0, lane_pad)).reshape(-1, 1).astype(F32),
            w_gate=w_gate[li].astype(BF16), w_up=w_up[li].astype(BF16), w_down=w_down[li].astype(BF16))

    cache_kt = jnp.transpose(cache_k, (0, 1, 3, 4, 2)).reshape(depth, dec_b, aw, past)
    pos_p = jnp.arange(x_prompt.shape[1])
    pos_s = past + jnp.arange(x_sample.shape[1])
    (bp, sp), (bs, ss) = x_prompt.shape[:2], x_sample.shape[:2]
    xp, xs = x_prompt.reshape(bp * sp, d), x_sample.reshape(bs * ss, d)
    kv_p, kv_s, st_p, st_s = [], [], [], []
    for li in range(depth):
        p = layer_params(li)
        xp, ka, va, st = _layer(xp, bp, sp, pos_p, None, None, p, li, dims)
        kv_p.append((ka.reshape(bp, sp, heads2, hda), va.reshape(bp, sp, heads2 // 2, 2 * hda)))
        st_p.append(st)
        xs, ka, va, st = _layer(xs, bs, ss, pos_s, (cache_kt, cache_v, li), state_ret[li], p, li, dims)
        kv_s.append((ka.reshape(bs, ss, heads2, hda), va.reshape(bs, ss, heads2 // 2, 2 * hda)))
        st_s.append(st)
    stack = lambda items: jnp.stack(list(items))
    return (xp.reshape(bp, sp, d), xs.reshape(bs, ss, d),
            stack(k for k, _ in kv_p), stack(v for _, v in kv_p), stack(st_p),
            stack(k for k, _ in kv_s), stack(v for _, v in kv_s), stack(st_s))
```

```python
import functools

import jax
import jax.numpy as jnp
import numpy as np
from jax import lax
from jax.experimental import pallas as pl
from jax.experimental.pallas import tpu as pltpu

F32 = jnp.float32
BF16 = jnp.bfloat16
I32 = jnp.int32

CHUNK = 64
ROPE_THETA = 10000.0
EPS = 1e-6
NEG = -1e30
LANE = 128
SUBLANES = 8
ROUTER_LANES = 128
VMEM_LIMIT_BYTES = 52 * 1024 * 1024

TOKEN_TILE = 512
ATTN_TILE = 256
CACHE_TILE = 1024
LOG2E = 1.4426950408889634
RET_CHUNK = 256
EXPERT_TILE = 256
SMALL_EXPERT_TILE = 16


def _cparams(*sem):
    return pltpu.CompilerParams(dimension_semantics=sem, vmem_limit_bytes=VMEM_LIMIT_BYTES)


def _rms(x, gain):
    ms = jnp.mean(x * x, axis=-1, keepdims=True)
    return x * lax.rsqrt(ms + EPS) * gain


def _rot_half(y, half):
    if 2 * half == LANE:
        return pltpu.roll(y, half, axis=1)
    lane = lax.broadcasted_iota(I32, y.shape, 1)
    first = (lane & (2 * half - 1)) < half
    return jnp.where(first, pltpu.roll(y, LANE - half, axis=1), pltpu.roll(y, half, axis=1))


def _row_tiled(t, d):
    assert d % (SUBLANES * LANE) == 0
    return (t, d // LANE, LANE)


def _rows_spec(ts, shape):
    zeros = (0,) * (len(shape) - 1)
    return pl.BlockSpec((ts, *shape[1:]), lambda i, *s: (i, *zeros))


def _retile_copies(buf_ref, slot, out_ref, row0, sem_ref):
    rows = buf_ref.shape[1]
    copies = []
    for c in range(out_ref.shape[1] if len(out_ref.shape) == 3 else buf_ref.shape[2]):
        lanes = pl.ds(c * LANE, LANE)
        if len(out_ref.shape) == 3:
            src, dst = buf_ref.at[slot, :, lanes], out_ref.at[pl.ds(row0, rows), c, :]
        else:
            src, dst = buf_ref.at[slot, :, c, :], out_ref.at[pl.ds(row0, rows), lanes]
        copies.append(pltpu.make_async_copy(src, dst, sem_ref.at[slot]))
    return copies


def _fetch_rows(src_ref, buf_ref, sem_ref, n_steps):
    step = pl.program_id(0)
    slot = lax.rem(step, 2)
    rows = buf_ref.shape[1]

    def copies(s, row0):
        return [pltpu.make_async_copy(src_ref.at[pl.ds(row0, rows), c, :],
                                      buf_ref.at[s, :, pl.ds(c * LANE, LANE)], sem_ref.at[s])
                for c in range(src_ref.shape[1])]

    @pl.when(step == 0)
    def _first():
        for cp in copies(0, 0):
            cp.start()

    @pl.when(step + 1 < n_steps)
    def _next():
        for cp in copies(1 - slot, pl.multiple_of((step + 1) * rows, rows)):
            cp.start()

    for cp in copies(slot, 0):
        cp.wait()
    return slot


def _emit_rows(buf_ref, out_ref, sem_ref, n_steps, fill):
    step = pl.program_id(0)
    slot = lax.rem(step, 2)
    rows = buf_ref.shape[1]

    def drain(s):
        for cp in _retile_copies(buf_ref, s, out_ref, 0, sem_ref):
            cp.wait()

    @pl.when(step >= 2)
    def _reuse():
        drain(slot)

    fill(slot)
    for cp in _retile_copies(buf_ref, slot, out_ref, pl.multiple_of(step * rows, rows), sem_ref):
        cp.start()

    @pl.when(step == n_steps - 1)
    def _finish():
        drain(slot)
        if n_steps >= 2:
            drain(1 - slot)


def _proj_kernel(x_ref, g_ref, w_ref, gm_ref, gq_ref, gk_ref, ca_ref, sa_ref, cb_ref, sb_ref,
                 qa_ref, ka_ref, va_ref, vabf_ref, qb_ref, kb_ref, vb_ref, gb_ref, vbuf_ref, vsem_ref,
                 *, n_steps, aw, bw, hda, dkb):
    h = _rms(x_ref[...], g_ref[...]).astype(BF16)

    def proj(c0, width):
        return jnp.dot(h, w_ref[:, c0:c0 + width], preferred_element_type=F32)

    def qk_norm_rot(z, gain_ref, out_ref):
        ss = jnp.dot((z * z).astype(BF16), gm_ref[...], preferred_element_type=F32)
        y = z * lax.rsqrt(ss + EPS) * gain_ref[...]
        for c in range(aw // LANE):
            yc = y[:, c * LANE:(c + 1) * LANE]
            r = yc * ca_ref[...] + _rot_half(yc, hda // 2) * sa_ref[...]
            out_ref[:, c * LANE:(c + 1) * LANE] = r.astype(out_ref.dtype)

    def rot_b(z, out_ref, scale):
        for c in range(bw // LANE):
            zc = z[:, c * LANE:(c + 1) * LANE]
            r = zc * cb_ref[...] + _rot_half(zc, dkb // 2) * sb_ref[...]
            out_ref[:, c * LANE:(c + 1) * LANE] = (r * scale).astype(out_ref.dtype)

    qk_norm_rot(proj(0, aw), gq_ref, qa_ref)
    qk_norm_rot(proj(aw, aw), gk_ref, ka_ref)
    va = proj(2 * aw, aw)
    vabf_ref[...] = va.astype(BF16)

    def fill(slot):
        vbuf_ref[slot] = va
    _emit_rows(vbuf_ref, va_ref, vsem_ref, n_steps, fill)
    rot_b(proj(3 * aw, bw), qb_ref, 1.0)
    rot_b(proj(3 * aw + bw, bw), kb_ref, dkb ** -0.5)
    vb_ref[...] = proj(3 * aw + 2 * bw, bw).astype(BF16)
    gb_ref[...] = proj(3 * aw + 3 * bw, bw).astype(BF16)


def _project(x2d, seq, gain, w_bf, gm, gq_t, gk_t, tabs, *, aw, bw, hda, dkb):
    t, d = x2d.shape[0], gain.shape[1]
    ts = min(TOKEN_TILE, seq)
    n_s = seq // ts
    row = lambda i: (i, 0)
    const = lambda i: (0, 0)
    tab = lambda i: (i % n_s, 0)
    tab_spec = pl.BlockSpec((ts, LANE), tab)
    out_a = pl.BlockSpec((ts, aw), row)
    out_b = pl.BlockSpec((ts, bw), row)
    return pl.pallas_call(
        functools.partial(_proj_kernel, n_steps=t // ts, aw=aw, bw=bw, hda=hda, dkb=dkb),
        grid=(t // ts,),
        in_specs=[pl.BlockSpec((ts, d), row), pl.BlockSpec((1, d), const),
                  pl.BlockSpec(w_bf.shape, const), pl.BlockSpec(gm.shape, const),
                  pl.BlockSpec((1, aw), const), pl.BlockSpec((1, aw), const),
                  tab_spec, tab_spec, tab_spec, tab_spec],
        out_specs=[out_a, out_a, pl.BlockSpec(memory_space=pl.ANY), out_a, out_b, out_b, out_b, out_b],
        out_shape=[jax.ShapeDtypeStruct((t, aw), BF16), jax.ShapeDtypeStruct((t, aw), F32),
                   jax.ShapeDtypeStruct((t, aw // LANE, LANE), F32), jax.ShapeDtypeStruct((t, aw), BF16),
                   jax.ShapeDtypeStruct((t, bw), BF16), jax.ShapeDtypeStruct((t, bw), BF16),
                   jax.ShapeDtypeStruct((t, bw), BF16), jax.ShapeDtypeStruct((t, bw), BF16)],
        scratch_shapes=[pltpu.VMEM((2, ts, aw), F32), pltpu.SemaphoreType.DMA((2,))],
        compiler_params=_cparams("arbitrary"),
        name="proj",
    )(x2d, gain, w_bf, gm, gq_t, gk_t, *tabs)


def _scaled_queries(q, hda):
    return (q.astype(F32) * (hda ** -0.5 * LOG2E)).astype(BF16)


def _split_halves(k, hda):
    lo = (lax.broadcasted_iota(I32, k.shape, 1) & (LANE - 1)) < hda
    return jnp.where(lo, k, 0.0).astype(BF16), jnp.where(lo, 0.0, k).astype(BF16)


def _reset_softmax_state(m_ref, l_ref, acc_ref):
    m_ref[...] = jnp.full(m_ref.shape, NEG, F32)
    l_ref[...] = jnp.zeros(l_ref.shape, F32)
    acc_ref[...] = jnp.zeros(acc_ref.shape, F32)


def _softmax_blocks(qs, blocks, m_ref, l_ref, acc_ref):
    scores = [[lax.dot_general(kk, q, (((1,), (1,)), ((), ())), preferred_element_type=F32)
               for q, halves in zip(qs, k_halves) for kk in halves] for k_halves, _, _ in blocks]
    for (_, vts, valid), block_scores in zip(blocks, scores):
        probs, alphas = [], []
        for c, s in enumerate(block_scores):
            if valid is not None:
                s = jnp.where(valid, s, NEG)
            m_old = m_ref[c]
            m_new = jnp.maximum(m_old, jnp.max(s, axis=0, keepdims=True))
            alpha = jnp.exp2(m_old - m_new)
            p = jnp.exp2(s - m_new)
            l_ref[c] = alpha * l_ref[c] + jnp.sum(p, axis=0, keepdims=True)
            m_ref[c] = m_new
            probs.append(p.astype(BF16))
            alphas.append(alpha)
        for c, (p, alpha) in enumerate(zip(probs, alphas)):
            acc_ref[c] = alpha * acc_ref[c] + jnp.dot(vts[c // 2], p, preferred_element_type=F32)


def _diff_lambda(lv_ref, lam_init):
    lv = lv_ref[...]
    return (jnp.exp(jnp.sum(lv[0:1] * lv[1:2], axis=1, keepdims=True))
            - jnp.exp(jnp.sum(lv[2:3] * lv[3:4], axis=1, keepdims=True)) + lam_init)


def _finish_heads(lv_ref, sg_ref, o_ref, l_ref, acc_ref, lam_init):
    lam = _diff_lambda(lv_ref, lam_init)
    for h in range(o_ref.shape[2] // LANE):
        o_t = acc_ref[2 * h] / l_ref[2 * h] - lam * (acc_ref[2 * h + 1] / l_ref[2 * h + 1])
        o_ref[0, :, h * LANE:(h + 1) * LANE] = (_rms(o_t.T, sg_ref[...]) * (1.0 - lam_init)).astype(o_ref.dtype)


def _attn_prompt_kernel(q_ref, k_ref, v_ref, lv_ref, sg_ref, o_ref,
                        qs_ref, klo_ref, khi_ref, vt_ref, m_ref, l_ref, acc_ref, *, tile, hda, lam_init):
    qi = pl.program_id(1)
    n_pairs = q_ref.shape[2] // LANE
    n_blk = k_ref.shape[1] // tile

    @pl.when(qi == 0)
    def _stage_keys():
        def stage(j, carry):
            rows = pl.ds(pl.multiple_of(j * tile, tile), tile)
            klo_ref[rows, :], khi_ref[rows, :] = _split_halves(k_ref[0, rows, :], hda)
            v = v_ref[0, rows, :]
            for h in range(n_pairs):
                vt_ref[j * n_pairs + h] = v[:, h * LANE:(h + 1) * LANE].astype(F32).T.astype(BF16)
            return carry
        lax.fori_loop(0, n_blk, stage, 0)

    qs_ref[...] = _scaled_queries(q_ref[0], hda)
    _reset_softmax_state(m_ref, l_ref, acc_ref)

    cols = [slice(h * LANE, (h + 1) * LANE) for h in range(n_pairs)]

    def key_block(kb, valid):
        rows = pl.ds(pl.multiple_of(kb * tile, tile), tile)
        return ([(klo_ref[rows, c], khi_ref[rows, c]) for c in cols],
                [vt_ref[kb * n_pairs + h] for h in range(n_pairs)], valid)

    def attend(blocks):
        _softmax_blocks([qs_ref[:, c] for c in cols], blocks, m_ref, l_ref, acc_ref)

    def past_pair(j, carry):
        attend([key_block(2 * j, None), key_block(2 * j + 1, None)])
        return carry
    lax.fori_loop(0, qi // 2, past_pair, 0)
    shift = CHUNK.bit_length() - 1
    kchunk = lax.broadcasted_iota(I32, (tile, tile), 0) >> shift
    qchunk = lax.broadcasted_iota(I32, (tile, tile), 1) >> shift
    own = kchunk <= qchunk

    @pl.when(qi % 2 == 1)
    def _odd():
        attend([key_block(qi - 1, None), key_block(qi, own)])

    @pl.when(qi % 2 == 0)
    def _even():
        attend([key_block(qi, own)])
    _finish_heads(lv_ref, sg_ref, o_ref, l_ref, acc_ref, lam_init)


def _attn_cached_kernel(q_ref, kct_ref, vc_ref, kn_ref, vn_ref, lv_ref, sg_ref, o_ref,
                        qlo_ref, qhi_ref, m_ref, l_ref, acc_ref, *, hda, lam_init):
    pi = pl.program_id(1)
    n_pairs = q_ref.shape[2] // LANE
    cols = [slice(h * LANE, (h + 1) * LANE) for h in range(n_pairs)]

    @pl.when(pi == 0)
    def _start():
        qlo_ref[...], qhi_ref[...] = _split_halves(_scaled_queries(q_ref[0], hda), hda)
        _reset_softmax_state(m_ref, l_ref, acc_ref)

    def update(c, s, v):
        m_old = m_ref[c]
        m_new = jnp.maximum(m_old, jnp.max(s, axis=1, keepdims=True))
        alpha = jnp.exp2(m_old - m_new)
        p = jnp.exp2(s - m_new)
        l_ref[c] = alpha * l_ref[c] + jnp.sum(p, axis=1, keepdims=True)
        acc_ref[c] = alpha * acc_ref[c] + jnp.dot(p.astype(BF16), v, preferred_element_type=F32)
        m_ref[c] = m_new

    def attend(score_fn, value_fn):
        scores = [score_fn(q[:, c], h) for h, c in enumerate(cols) for q in (qlo_ref, qhi_ref)]
        for h in range(n_pairs):
            v = value_fn(h)
            update(2 * h, scores[2 * h], v)
            update(2 * h + 1, scores[2 * h + 1], v)

    attend(lambda q, h: jnp.dot(q, kct_ref[0, cols[h], :].astype(BF16), preferred_element_type=F32),
           lambda h: vc_ref[0, :, h, :].astype(BF16))

    @pl.when(pi == pl.num_programs(1) - 1)
    def _finish():
        attend(lambda q, h: lax.dot_general(q, kn_ref[0, :, cols[h]].astype(BF16), (((1,), (1,)), ((), ())),
                                            preferred_element_type=F32),
               lambda h: vn_ref[0, :, cols[h]])
        lam = _diff_lambda(lv_ref, lam_init)
        for h, c in enumerate(cols):
            o = acc_ref[2 * h] / l_ref[2 * h] - lam * (acc_ref[2 * h + 1] / l_ref[2 * h + 1])
            o_ref[0, :, c] = (_rms(o, sg_ref[...]) * (1.0 - lam_init)).astype(o_ref.dtype)


def _diff_attention(qa, k_all, k_layer, va, cache, lam_vec, subln, *, causal, hda, lam_init):
    b, s, w = qa.shape
    n_chains = 2 * (w // LANE)
    const = lambda bi, si: (0, 0)
    whole = pl.BlockSpec((1, s, w), lambda bi, si: (bi, 0, 0))
    whole_k = pl.BlockSpec((None, 1, s, w), lambda bi, si: (k_layer, bi, 0, 0))
    small = [pl.BlockSpec(lam_vec.shape, const), pl.BlockSpec((1, LANE), const)]

    def state(tq):
        return [pltpu.VMEM((tq, w), BF16), pltpu.VMEM((n_chains, 1, tq), F32),
                pltpu.VMEM((n_chains, 1, tq), F32), pltpu.VMEM((n_chains, LANE, tq), F32)]

    if causal:
        tile = min(ATTN_TILE, s)
        q_blk = pl.BlockSpec((1, tile, w), lambda bi, qi: (bi, qi, 0))
        qs, *softmax_state = state(tile)
        return pl.pallas_call(
            functools.partial(_attn_prompt_kernel, tile=tile, hda=hda, lam_init=lam_init),
            grid=(b, s // tile),
            in_specs=[q_blk, whole_k, whole] + small,
            out_specs=q_blk,
            out_shape=jax.ShapeDtypeStruct((b, s, w), BF16),
            scratch_shapes=[qs, pltpu.VMEM((s, w), BF16), pltpu.VMEM((s, w), BF16),
                            pltpu.VMEM((s // tile * (w // LANE), LANE, tile), BF16)] + softmax_state,
            compiler_params=_cparams("parallel", "arbitrary"),
            name="diff_attn",
        )(qa, k_all, va, lam_vec, subln)
    ckt, cv, li = cache
    past = ckt.shape[3]
    tkc = min(CACHE_TILE, past)
    return pl.pallas_call(
        functools.partial(_attn_cached_kernel, hda=hda, lam_init=lam_init),
        grid=(b, past // tkc),
        in_specs=[whole, pl.BlockSpec((None, 1, w, tkc), lambda bi, pi: (li, bi, 0, pi)),
                  pl.BlockSpec((None, 1, tkc, *cv.shape[3:]), lambda bi, pi: (li, bi, pi, 0, 0)),
                  whole_k, whole] + small,
        out_specs=whole,
        out_shape=jax.ShapeDtypeStruct((b, s, w), BF16),
        scratch_shapes=[pltpu.VMEM((s, w), BF16), pltpu.VMEM((s, w), BF16), pltpu.VMEM((n_chains, s, 1), F32),
                        pltpu.VMEM((n_chains, s, 1), F32), pltpu.VMEM((n_chains, s, LANE), F32)],
        compiler_params=_cparams("parallel", "arbitrary"),
        name="diff_attn_cached",
    )(qa, ckt, cv, k_all, va, lam_vec, subln)


def _retention_kernel(*refs, has_state, chunk):
    if has_state:
        q_ref, k_ref, v_ref, g_ref, rg_ref, s0_ref, y_ref, so_ref, st_ref, decay_ref = refs
    else:
        q_ref, k_ref, v_ref, g_ref, rg_ref, y_ref, so_ref, st_ref, decay_ref = refs
    ci = pl.program_id(1)
    n_heads = q_ref.shape[2] // LANE
    log_gammas = [float(np.log(1.0 - 2.0 ** (-5.0 - h))) for h in range(n_heads)]

    @pl.when(ci == 0)
    def _init():
        st_ref[...] = s0_ref[0] if has_state else jnp.zeros(st_ref.shape, F32)
        row = lax.broadcasted_iota(I32, (chunk, chunk), 0)
        col = lax.broadcasted_iota(I32, (chunk, chunk), 1)
        dist = (row - col).astype(F32)
        for h, lg in enumerate(log_gammas):
            decay_ref[h] = jnp.where(dist >= 0, jnp.exp(lg * jnp.maximum(dist, 0.0)), 0.0)

    idx = lax.broadcasted_iota(I32, (chunk, 1), 0).astype(F32)
    heads = list(enumerate(log_gammas))
    cols = [slice(h * LANE, (h + 1) * LANE) for h in range(n_heads)]
    qk = [lax.dot_general(q_ref[0, :, c], k_ref[0, :, c], (((1,), (1,)), ((), ())), preferred_element_type=F32)
          for c in cols]
    cross = [jnp.dot(q_ref[0, :, c], st_ref[h].astype(BF16), preferred_element_type=F32)
             for h, c in enumerate(cols)]
    kv = []
    for (h, lg), c in zip(heads, cols):
        kz = (k_ref[0, :, c].astype(F32) * jnp.exp(lg * (chunk - 1.0 - idx))).astype(BF16)
        kv.append(lax.dot_general(kz, v_ref[0, :, c], (((0,), (0,)), ((), ())), preferred_element_type=F32))
    intra = [jnp.dot((qk[h] * decay_ref[h]).astype(BF16), v_ref[0, :, c], preferred_element_type=F32)
             for h, c in enumerate(cols)]
    for (h, lg), c in zip(heads, cols):
        out = intra[h] + cross[h] * jnp.exp(lg * (idx + 1.0))
        st_ref[h] = float(np.exp(lg * chunk)) * st_ref[h] + kv[h]
        gate = g_ref[0, :, c].astype(F32)
        y_ref[0, :, c] = (_rms(out, rg_ref[...]) * (gate * jax.nn.sigmoid(gate))).astype(y_ref.dtype)

    @pl.when(ci == pl.num_programs(1) - 1)
    def _emit_state():
        so_ref[0] = st_ref[...]


def _retention(qb, kb, vb, gb, ret_g, state0):
    b, s, w = qb.shape
    n_heads = w // LANE
    chunk = min(RET_CHUNK, s)
    has_state = state0 is not None
    blk = pl.BlockSpec((1, chunk, w), lambda bi, ci: (bi, ci, 0))
    st_spec = pl.BlockSpec((1, n_heads, LANE, LANE), lambda bi, ci: (bi, 0, 0, 0))
    in_specs = [blk, blk, blk, blk, pl.BlockSpec((1, LANE), lambda bi, ci: (0, 0))]
    args = [qb, kb, vb, gb, ret_g]
    if has_state:
        in_specs.append(st_spec)
        args.append(state0)
    return pl.pallas_call(
        functools.partial(_retention_kernel, has_state=has_state, chunk=chunk),
        grid=(b, s // chunk),
        in_specs=in_specs,
        out_specs=[blk, st_spec],
        out_shape=[jax.ShapeDtypeStruct((b, s, w), BF16), jax.ShapeDtypeStruct((b, n_heads, LANE, LANE), F32)],
        scratch_shapes=[pltpu.VMEM((n_heads, LANE, LANE), F32), pltpu.VMEM((n_heads, chunk, chunk), F32)],
        compiler_params=_cparams("parallel", "arbitrary"),
        name="retention",
    )(*args)


def _router_logits(h_bf, wr_ref, br_ref):
    return jnp.dot(h_bf, wr_ref[...], preferred_element_type=F32) + br_ref[...]


def _first_row_of_max(vals, row):
    vmax = jnp.max(vals, axis=0, keepdims=True)
    first = jnp.min(jnp.where(vals == vmax, row.astype(F32), float(ROUTER_LANES)), axis=0, keepdims=True)
    return first.astype(I32)


def _mix_kernel(x_ref, ya_ref, yb_ref, woa_ref, wob_ref, g_ref, wrt_ref, brt_ref,
                x1_ref, cls_ref, cnt_ref, buf_ref, sem_ref, *, n_steps, n_groups, n_experts):
    x1 = (x_ref[...] + jnp.dot(ya_ref[...], woa_ref[...], preferred_element_type=F32)
          + jnp.dot(yb_ref[...], wob_ref[...], preferred_element_type=F32))

    def fill(slot):
        buf_ref[slot] = x1
    _emit_rows(buf_ref, x1_ref, sem_ref, n_steps, fill)
    h = _rms(x1, g_ref[...]).astype(BF16)
    logits = lax.dot_general(wrt_ref[...], h, (((1,), (1,)), ((), ())), preferred_element_type=F32) + brt_ref[...]
    row = lax.broadcasted_iota(I32, logits.shape, 0)
    g_sel = _first_row_of_max(jnp.where(row < n_groups, logits, NEG), row)
    e_lo = n_groups + g_sel * n_experts
    e_logits = jnp.where((row >= e_lo) & (row < e_lo + n_experts), logits, NEG)
    top1 = _first_row_of_max(e_logits, row)
    top2 = _first_row_of_max(jnp.where(row == top1, NEG, e_logits), row)
    ea = jnp.minimum(top1, top2) - e_lo
    eb = jnp.maximum(top1, top2) - e_lo
    pair = ((ea * (2 * n_experts - 1 - ea)) >> 1) + (eb - ea - 1)
    n_pairs = n_experts * (n_experts - 1) // 2
    cls = g_sel * n_pairs + pair
    cls_ref[0] = cls

    @pl.when(pl.program_id(0) == 0)
    def _zero():
        cnt_ref[...] = jnp.zeros(cnt_ref.shape, F32)

    cnt_ref[...] += jnp.sum(jnp.where(row == cls, 1.0, 0.0), axis=1, keepdims=True)


def _mix(x2d, ya, yb, wo_a, wo_b, gain, wrt, brt, *, n_groups, n_experts):
    t, d = x2d.shape[0], gain.shape[1]
    ts = min(TOKEN_TILE, t)
    row = lambda i: (i, 0)
    const = lambda i: (0, 0)
    x1_shape = _row_tiled(t, d)
    return pl.pallas_call(
        functools.partial(_mix_kernel, n_steps=t // ts, n_groups=n_groups, n_experts=n_experts),
        grid=(t // ts,),
        in_specs=[pl.BlockSpec((ts, d), row), pl.BlockSpec((ts, ya.shape[1]), row),
                  pl.BlockSpec((ts, yb.shape[1]), row), pl.BlockSpec(wo_a.shape, const),
                  pl.BlockSpec(wo_b.shape, const), pl.BlockSpec((1, d), const),
                  pl.BlockSpec(wrt.shape, const), pl.BlockSpec(brt.shape, const)],
        out_specs=[pl.BlockSpec(memory_space=pl.ANY), pl.BlockSpec((1, 1, ts), lambda i: (i, 0, 0)),
                   pl.BlockSpec((ROUTER_LANES, 1), const)],
        out_shape=[jax.ShapeDtypeStruct(x1_shape, F32), jax.ShapeDtypeStruct((t // ts, 1, ts), I32),
                   jax.ShapeDtypeStruct((ROUTER_LANES, 1), F32)],
        scratch_shapes=[pltpu.VMEM((2, ts, d), F32), pltpu.SemaphoreType.DMA((2,))],
        compiler_params=_cparams("arbitrary"),
        name="mix",
    )(x2d, ya, yb, wo_a, wo_b, gain, wrt, brt)


ISSUE_UNROLL = 8
DMA_LANES = 2


def _row_copy(src, src_row, dst, dst_row, sem):
    return pltpu.make_async_copy(src.at[src_row], dst.at[dst_row], sem)


def _rows_copy(src, dst, n, sem):
    return pltpu.make_async_copy(src.at[pl.ds(0, n)], dst.at[pl.ds(0, n)], sem)


def _slot_kernel(cls_ref, start_ref, pos_ref, next_ref, earlier_ref):
    ts = cls_ref.shape[2]

    @pl.when(pl.program_id(0) == 0)
    def _init():
        next_ref[...] = start_ref[...]
        before = lax.broadcasted_iota(I32, (ts, ts), 0) < lax.broadcasted_iota(I32, (ts, ts), 1)
        earlier_ref[...] = jnp.where(before, 1.0, 0.0).astype(BF16)

    member = lax.broadcasted_iota(I32, (ROUTER_LANES, ts), 0) == cls_ref[0]
    onehot = jnp.where(member, 1.0, 0.0)
    ahead = jnp.dot(onehot.astype(BF16), earlier_ref[...], preferred_element_type=F32)
    pos_ref[0] = jnp.sum(onehot * (ahead + next_ref[...]), axis=0, keepdims=True).astype(I32)
    next_ref[...] += jnp.sum(onehot, axis=1, keepdims=True)


def _slots(cls, slot_start):
    n_tiles, _, ts = cls.shape
    blk = pl.BlockSpec((1, 1, ts), lambda i: (i, 0, 0))
    return pl.pallas_call(
        _slot_kernel,
        grid=(n_tiles,),
        in_specs=[blk, pl.BlockSpec((ROUTER_LANES, 1), lambda i: (0, 0))],
        out_specs=blk,
        out_shape=jax.ShapeDtypeStruct(cls.shape, I32),
        scratch_shapes=[pltpu.VMEM((ROUTER_LANES, 1), F32), pltpu.VMEM((ts, ts), BF16)],
        compiler_params=_cparams("arbitrary"),
        name="slots",
    )(cls, slot_start.astype(F32).reshape(ROUTER_LANES, 1))


def _dispatch_kernel(fill_ref, end_ref, pos_ref, x_ref, xs_ref, zero_ref, sem, *, n_cls):
    step = pl.program_id(0)
    ts = x_ref.shape[0]

    @pl.when(step == 0)
    def _init():
        zero_ref[...] = jnp.zeros(zero_ref.shape, F32)

    def place(j, carry):
        for lane in range(DMA_LANES):
            r = j * DMA_LANES + lane
            _row_copy(x_ref, r, xs_ref, pos_ref[0, 0, r], sem).start(priority=lane)
        return carry
    lax.fori_loop(0, ts // DMA_LANES, place, 0, unroll=ISSUE_UNROLL // DMA_LANES)
    _rows_copy(x_ref, xs_ref, ts, sem).wait()

    @pl.when(step == pl.num_programs(0) - 1)
    def _pad():
        tile = zero_ref.shape[0]

        def pad_copies(c, act):
            row, n = fill_ref[c], end_ref[c] - fill_ref[c]
            p = tile // 2
            while p:
                piece = pltpu.make_async_copy(zero_ref.at[pl.ds(0, p)], xs_ref.at[pl.ds(row, p)], sem)
                pl.when((n & p) != 0)(functools.partial(act, piece))
                row = row + (n & p)
                p //= 2

        def start_class(c, carry):
            pad_copies(c, lambda piece: piece.start())
            return carry
        lax.fori_loop(0, n_cls, start_class, 0)

        def wait_class(c, carry):
            pad_copies(c, lambda piece: piece.wait())
            return carry
        lax.fori_loop(0, n_cls, wait_class, 0)

        first_free = lax.div(end_ref[n_cls - 1], tile)
        n_tiles = xs_ref.shape[0] // tile

        def tile_copy(ti):
            return pltpu.make_async_copy(zero_ref, xs_ref.at[pl.ds(pl.multiple_of(ti * tile, tile), tile)], sem)

        def clear(ti, carry):
            tile_copy(ti).start()
            return carry
        lax.fori_loop(first_free, n_tiles, clear, 0)

        def clear_wait(ti, carry):
            tile_copy(ti).wait()
            return carry
        lax.fori_loop(first_free, n_tiles, clear_wait, 0)


def _dispatch(x1, pos, fill_from, slot_end, n_slots, *, n_cls, tile):
    n_tiles, _, ts = pos.shape
    return pl.pallas_call(
        functools.partial(_dispatch_kernel, n_cls=n_cls),
        grid_spec=pltpu.PrefetchScalarGridSpec(
            num_scalar_prefetch=2,
            grid=(n_tiles,),
            in_specs=[pl.BlockSpec((1, 1, ts), lambda i, *_: (i, 0, 0), memory_space=pltpu.SMEM),
                      _rows_spec(ts, x1.shape)],
            out_specs=pl.BlockSpec(memory_space=pl.ANY),
            scratch_shapes=[pltpu.VMEM((tile, *x1.shape[1:]), F32), pltpu.SemaphoreType.DMA],
        ),
        out_shape=jax.ShapeDtypeStruct((n_slots, *x1.shape[1:]), F32),
        compiler_params=_cparams("arbitrary"),
        name="dispatch",
    )(fill_from, slot_end, pos, x1)


def _gather_rows(pos_ref, ys_ref, rows_ref, sem):
    ts = rows_ref.shape[0]

    def fetch(j, carry):
        for lane in range(DMA_LANES):
            r = j * DMA_LANES + lane
            _row_copy(ys_ref, pos_ref[0, 0, r], rows_ref, r, sem).start(priority=lane)
        return carry
    lax.fori_loop(0, ts // DMA_LANES, fetch, 0, unroll=ISSUE_UNROLL // DMA_LANES)
    _rows_copy(ys_ref, rows_ref, ts, sem).wait()


def _unpermute_kernel(pos_ref, ys_ref, o_ref, stage_ref, sem, out_sem, *, n_steps):
    def fill(slot):
        _gather_rows(pos_ref, ys_ref, stage_ref.at[slot], sem)
    _emit_rows(stage_ref, o_ref, out_sem, n_steps, fill)


def _unpermute(ys, pos, t):
    n_tiles, _, ts = pos.shape
    hbm = pl.BlockSpec(memory_space=pl.ANY)
    return pl.pallas_call(
        functools.partial(_unpermute_kernel, n_steps=n_tiles),
        grid=(n_tiles,),
        in_specs=[pl.BlockSpec((1, 1, ts), lambda i: (i, 0, 0), memory_space=pltpu.SMEM), hbm],
        out_specs=hbm,
        out_shape=jax.ShapeDtypeStruct((t, ys.shape[1] * ys.shape[2]), F32),
        scratch_shapes=[pltpu.VMEM((2, ts, *ys.shape[1:]), F32), pltpu.SemaphoreType.DMA,
                        pltpu.SemaphoreType.DMA((2,))],
        compiler_params=_cparams("arbitrary"),
        name="unpermute",
    )(pos, ys)


def _expert_kernel(tg_ref, ta_ref, tb_ref, nv_ref, xs_ref, g_ref, wr_ref, br_ref,
                   wga_ref, wua_ref, wda_ref, wgb_ref, wub_ref, wdb_ref, ys_ref,
                   xbuf_ref, in_sem_ref, buf_ref, sem_ref, *, n_steps, n_groups, n_experts):
    i = pl.program_id(0)
    in_slot = _fetch_rows(xs_ref, xbuf_ref, in_sem_ref, n_steps)

    def run():
        x = xbuf_ref[in_slot]
        h = _rms(x, g_ref[...]).astype(BF16)
        logits = _router_logits(h, wr_ref, br_ref)
        gates = [jnp.dot(h, w[0, 0], preferred_element_type=F32) for w in (wga_ref, wgb_ref)]
        ups = [jnp.dot(h, w[0, 0], preferred_element_type=F32) for w in (wua_ref, wub_ref)]
        lane = lax.broadcasted_iota(I32, logits.shape, 1)
        grp, ea, eb = tg_ref[i], ta_ref[i], tb_ref[i]

        def pick(idx):
            return jnp.sum(jnp.where(lane == idx, logits, 0.0), axis=1, keepdims=True)

        g_logits = jnp.where(lane < n_groups, logits, NEG)
        g_max = jnp.max(g_logits, axis=1, keepdims=True)
        g_w = jnp.exp(pick(grp) - g_max) / jnp.sum(jnp.exp(g_logits - g_max), axis=1, keepdims=True)
        e_lo = n_groups + grp * n_experts
        w_a = jax.nn.sigmoid(pick(e_lo + ea) - pick(e_lo + eb))
        acts = [(gate * jax.nn.sigmoid(gate) * up * comb).astype(BF16)
                for gate, up, comb in zip(gates, ups, (g_w * w_a, g_w * (1.0 - w_a)))]
        return (x + jnp.dot(acts[0], wda_ref[0, 0], preferred_element_type=F32)
                + jnp.dot(acts[1], wdb_ref[0, 0], preferred_element_type=F32))

    def fill(slot):
        @pl.when(nv_ref[i] == 0)
        def _unused_tile():
            buf_ref[slot] = jnp.zeros(buf_ref.shape[1:], F32)

        @pl.when(nv_ref[i] > 0)
        def _used_tile():
            buf_ref[slot] = run()
    _emit_rows(buf_ref, ys_ref, sem_ref, n_steps, fill)


def _experts(xs, gain, wr, br, wg, wu, wd, meta, *, tile, n_groups, n_experts):
    n_slots, d = xs.shape[0], gain.shape[1]
    de = wg.shape[-1]
    n_steps = n_slots // tile
    tile_g, tile_a, tile_b, tile_nv = meta
    const = lambda i, *_: (0, 0)
    sel_a = lambda i, tg, ta, tb, nv: (tg[i], ta[i], 0, 0)
    sel_b = lambda i, tg, ta, tb, nv: (tg[i], tb[i], 0, 0)
    up_spec_a = pl.BlockSpec((1, 1, d, de), sel_a)
    up_spec_b = pl.BlockSpec((1, 1, d, de), sel_b)
    hbm = pl.BlockSpec(memory_space=pl.ANY)
    return pl.pallas_call(
        functools.partial(_expert_kernel, n_steps=n_steps, n_groups=n_groups, n_experts=n_experts),
        grid_spec=pltpu.PrefetchScalarGridSpec(
            num_scalar_prefetch=4,
            grid=(n_steps,),
            in_specs=[hbm, pl.BlockSpec((1, d), const),
                      pl.BlockSpec(wr.shape, const), pl.BlockSpec(br.shape, const),
                      up_spec_a, up_spec_a, pl.BlockSpec((1, 1, de, d), sel_a),
                      up_spec_b, up_spec_b, pl.BlockSpec((1, 1, de, d), sel_b)],
            out_specs=hbm,
            scratch_shapes=[pltpu.VMEM((2, tile, d), F32), pltpu.SemaphoreType.DMA((2,)),
                            pltpu.VMEM((2, tile, d), F32), pltpu.SemaphoreType.DMA((2,))],
        ),
        out_shape=jax.ShapeDtypeStruct(xs.shape, F32),
        compiler_params=_cparams("arbitrary"),
        name="experts",
    )(tile_g, tile_a, tile_b, tile_nv, xs, gain, wr, br, wg, wu, wd, wg, wu, wd)


def _class_layout(counts, n_cls, n_pairs, n_experts, n_tiles, tile):
    counts = counts[:n_cls, 0].astype(I32)
    tiles_c = (counts + tile - 1) // tile
    tile_end = jnp.cumsum(tiles_c)
    tile_start = tile_end - tiles_c
    n_active = tile_end[-1]
    pad = ROUTER_LANES - n_cls
    slot_start = jnp.pad(tile_start * tile, (0, pad))
    slot_end = jnp.pad(tile_end * tile, (0, pad))
    fill_from = jnp.pad(tile_start * tile + counts, (0, pad))
    t_idx = jnp.arange(n_tiles, dtype=I32)
    t_cls = jnp.minimum(jnp.sum((tile_end[None, :] <= t_idx[:, None]).astype(I32), axis=1), n_cls - 1)
    active = t_idx < n_active
    t_nv = jnp.where(active, jnp.clip(counts[t_cls] - (t_idx - tile_start[t_cls]) * tile, 0, tile), 0)
    t_cls = jnp.where(active, t_cls, t_cls[jnp.maximum(n_active - 1, 0)])
    pair_a, pair_b = np.triu_indices(n_experts, 1)
    pair = t_cls % n_pairs
    meta = (t_cls // n_pairs, jnp.asarray(pair_a, I32)[pair], jnp.asarray(pair_b, I32)[pair], t_nv.astype(I32))
    return slot_start.astype(I32), fill_from.astype(I32), slot_end.astype(I32), meta


def _rope_tables(pos, group, signed_half):
    half = group // 2
    lane = np.arange(LANE)
    inv = (ROPE_THETA ** (-jnp.arange(0, group, 2, dtype=F32) / group))[lane % half]
    ang = pos.astype(F32)[:, None] * inv[None, :]
    sign = np.where((lane % group) < signed_half, -1.0, 1.0).astype(np.float32)
    return jnp.cos(ang), jnp.sin(ang) * sign[None, :]


def _layer(x2d, b, s, pos, cache, state0, p, li, dims):
    aw, bw, hda, dkb = dims["aw"], dims["bw"], dims["hda"], dims["dkb"]
    n_groups, n_experts = dims["n_groups"], dims["n_experts"]
    lam_init = 0.8 - 0.6 * float(np.exp(-0.3 * li))
    t = b * s
    tabs = (*_rope_tables(pos, hda, hda // 2), *_rope_tables(pos, dkb, dkb // 2))
    qa, ka, va, va_bf, qb, kb, vb, gb = _project(x2d, s, p["norm_attn"], p["w_in"], p["gm"], p["gq"], p["gk"], tabs,
                                                 aw=aw, bw=bw, hda=hda, dkb=dkb)
    shape3 = lambda a: a.reshape(b, s, a.shape[-1])
    ya = _diff_attention(shape3(qa), ka.reshape(1, b, s, aw), 0, shape3(va_bf), cache,
                         p["lam_vec"], p["subln"], causal=cache is None, hda=hda, lam_init=lam_init)
    yb, new_state = _retention(shape3(qb), shape3(kb), shape3(vb), shape3(gb), p["ret_norm"], state0)
    x1, cls, counts = _mix(x2d, ya.reshape(t, aw), yb.reshape(t, bw), p["wo_a"], p["wo_b"], p["norm_ffn"],
                           p["wrt"], p["brt"], n_groups=n_groups, n_experts=n_experts)
    n_pairs = n_experts * (n_experts - 1) // 2
    n_cls = n_groups * n_pairs
    tile = EXPERT_TILE if t >= EXPERT_TILE * n_cls else SMALL_EXPERT_TILE
    n_tiles = t // tile + n_cls
    slot_start, fill_from, slot_end, meta = _class_layout(counts, n_cls, n_pairs, n_experts, n_tiles, tile)
    slot_of = _slots(cls, slot_start)
    xs = _dispatch(x1, slot_of, fill_from, slot_end, n_tiles * tile, n_cls=n_cls, tile=tile)
    ys = _experts(xs, p["norm_ffn"], p["wr"], p["br"], p["w_gate"], p["w_up"], p["w_down"], meta,
                  tile=tile, n_groups=n_groups, n_experts=n_experts)
    return _unpermute(ys, slot_of, t), ka, va, new_state


def kernel(x_prompt, x_sample, cache_k, cache_v, state_ret, norm_attn, w_in, q_norm, k_norm, lam_vec,
           subln, ret_norm, w_out, norm_ffn, w_group, b_group, w_expert, b_expert, w_gate, w_up, w_down):
    depth, dec_b, past, heads2, hda = cache_k.shape
    _, _, n_ret, dkb, dvb = state_ret.shape
    n_groups, n_experts = w_gate.shape[1], w_gate.shape[2]
    aw, bw = heads2 * hda, n_ret * dkb
    assert 2 * hda == LANE and dkb == LANE and dvb == LANE and aw % LANE == 0
    assert n_groups * (1 + n_experts) <= ROUTER_LANES
    assert n_groups * n_experts * (n_experts - 1) // 2 <= ROUTER_LANES
    dims = dict(aw=aw, bw=bw, hda=hda, dkb=dkb, n_groups=n_groups, n_experts=n_experts)
    d = x_prompt.shape[-1]
    group_of = np.arange(aw) // hda
    gm = jnp.asarray((group_of[:, None] == group_of[None, :]) / hda, BF16)
    row = lambda v: v.reshape(1, -1).astype(F32)

    def layer_params(li):
        wr = jnp.concatenate([w_group[li], w_expert[li].reshape(d, n_groups * n_experts)], axis=1)
        br = jnp.concatenate([b_group[li], b_expert[li].reshape(-1)])
        lane_pad = ROUTER_LANES - wr.shape[1]
        return dict(
            norm_attn=row(norm_attn[li]), w_in=w_in[li].astype(BF16), gm=gm,
            gq=row(jnp.tile(q_norm[li], heads2)), gk=row(jnp.tile(k_norm[li], heads2)),
            lam_vec=lam_vec[li].astype(F32), subln=row(subln[li]), ret_norm=row(ret_norm[li]),
            wo_a=w_out[li, :aw].astype(BF16), wo_b=w_out[li, aw:].astype(BF16), norm_ffn=row(norm_ffn[li]),
            wr=jnp.pad(wr, ((0, 0), (0, lane_pad))).astype(BF16), br=row(jnp.pad(br, (0, lane_pad))),
            wrt=jnp.pad(wr, ((0, 0), (0, lane_pad))).T.astype(BF16),
            brt=jnp.pad(br, (0, lane_pad)).reshape(-1, 1).astype(F32),
            w_gate=w_gate[li].astype(BF16), w_up=w_up[li].astype(BF16), w_down=w_down[li].astype(BF16))

    cache_kt = jnp.transpose(cache_k, (0, 1, 3, 4, 2)).reshape(depth, dec_b, aw, past)
    pos_p = jnp.arange(x_prompt.shape[1])
    pos_s = past + jnp.arange(x_sample.shape[1])
    (bp, sp), (bs, ss) = x_prompt.shape[:2], x_sample.shape[:2]
    xp, xs = x_prompt.reshape(bp * sp, d), x_sample.reshape(bs * ss, d)
    kv_p, kv_s, st_p, st_s = [], [], [], []
    for li in range(depth):
        p = layer_params(li)
        xp, ka, va, st = _layer(xp, bp, sp, pos_p, None, None, p, li, dims)
        kv_p.append((ka.reshape(bp, sp, heads2, hda), va.reshape(bp, sp, heads2 // 2, 2 * hda)))
        st_p.append(st)
        xs, ka, va, st = _layer(xs, bs, ss, pos_s, (cache_kt, cache_v, li), state_ret[li], p, li, dims)
        kv_s.append((ka.reshape(bs, ss, heads2, hda), va.reshape(bs, ss, heads2 // 2, 2 * hda)))
        st_s.append(st)
    stack = lambda items: jnp.stack(list(items))
    return (xp.reshape(bp, sp, d), xs.reshape(bs, ss, d),
            stack(k for k, _ in kv_p), stack(v for _, v in kv_p), stack(st_p),
            stack(k for k, _ in kv_s), stack(v for _, v in kv_s), stack(st_s))
```
